```python
import math
import jax, jax.numpy as jnp
from jax import lax
import numpy as np

D_MODEL = 1024
BATCH = 4
SEQ = 4096
DEPTH = 2
DEC_BATCH = 32
DEC_SEQ = 4
PAST_LEN = 16384
PAGE_SIZE = 128

M_HEADS = 4
M_DH = 64
M_WIDTH = M_HEADS * M_DH
CONV_W = 4
M_CHUNK = 64
S_HEADS = 8
S_KV_HEADS = 2
S_REP = S_HEADS // S_KV_HEADS
S_DH = 64
S_WIDTH = S_HEADS * S_DH
IDX_HEADS = 8
IDX_DH = 64
TOPK_MAX = 256
DF_HEADS = 4
DF_DH = 32
DF_VDH = 2 * DF_DH
DF_WIDTH = DF_HEADS * DF_VDH
MIX_WIDTH = M_WIDTH + S_WIDTH + DF_WIDTH
D_FF = 4 * D_MODEL
ROPE_THETA = 10000.0
EPS = 1e-6
Q_BLOCK = 128
F32 = jnp.float32

IN_SPLIT = (
    ('m_q', M_WIDTH), ('m_k', M_WIDTH), ('m_v', M_WIDTH), ('m_o', M_WIDTH),
    ('m_i', M_HEADS), ('m_f', M_HEADS),
    ('s_q', S_HEADS * S_DH), ('s_k', S_KV_HEADS * S_DH), ('s_v', S_KV_HEADS * S_DH),
    ('ix_q', IDX_HEADS * IDX_DH), ('ix_k', IDX_DH), ('ix_w', IDX_HEADS),
    ('d_q', DF_HEADS * 2 * DF_DH), ('d_k', DF_HEADS * 2 * DF_DH), ('d_v', DF_WIDTH),
)
IN_WIDTH = (4 * M_WIDTH + 2 * M_HEADS + S_HEADS * S_DH + 2 * S_KV_HEADS * S_DH
            + IDX_HEADS * IDX_DH + IDX_DH + IDX_HEADS + 2 * DF_HEADS * 2 * DF_DH + DF_WIDTH)

kernel_name = 'hymba_mlstm_dsa_diffattn_step'


def rmsnorm(x, g):
    xf = x.astype(F32)
    y = xf * lax.rsqrt(jnp.mean(xf * xf, axis=-1, keepdims=True) + EPS)
    return (y * g.astype(F32)).astype(x.dtype)


def rope(x, pos):
    d = x.shape[-1]
    inv = ROPE_THETA ** (-jnp.arange(0, d, 2, dtype=F32) / d)
    ang = pos.astype(F32)[:, None] * inv[None, :]
    ang = ang.reshape(ang.shape[:1] + (1,) * (x.ndim - 3) + ang.shape[1:])
    cos, sin = jnp.cos(ang), jnp.sin(ang)
    x1, x2 = jnp.split(x.astype(F32), 2, axis=-1)
    return jnp.concatenate([x1 * cos - x2 * sin, x2 * cos + x1 * sin], axis=-1).astype(x.dtype)


def split_columns(z):
    out, off = {}, 0
    for name, n in IN_SPLIT:
        out[name] = z[..., off:off + n]
        off += n
    return out


def causal_conv(u, buf, w, b):
    T = u.shape[1]
    full = jnp.concatenate([buf.astype(u.dtype), u], axis=1)
    y = b + full[:, 0:T] * w[0]
    for j in range(1, CONV_W):
        y = y + full[:, j:j + T] * w[j]
    return y, full[:, T:]


def mlstm_chunk(state, q, k, v, ig, lf):
    C, n, m = state
    L = q.shape[2]
    b = jnp.cumsum(lf, axis=-1)
    a = b + m[..., None]
    D = b[..., :, None] - b[..., None, :] + ig[..., None, :]
    D = jnp.where(jnp.tril(jnp.ones((L, L), bool)), D, -jnp.inf)
    m_t = jnp.maximum(a, jnp.max(D, axis=-1))
    w_inter = jnp.exp(a - m_t)
    w_intra = jnp.exp(D - m_t[..., None])
    qk = jnp.einsum('bhtd,bhsd->bhts', q, k) * w_intra
    num = jnp.einsum('bhts,bhsd->bhtd', qk, v) + w_inter[..., None] * jnp.einsum('bhvk,bhtk->bhtv', C, q)
    den = jnp.sum(qk, axis=-1) + w_inter * jnp.einsum('bhk,bhtk->bht', n, q)
    h = num / jnp.maximum(jnp.abs(den), jnp.exp(-m_t))[..., None]
    wl_inter = w_inter[..., -1]
    wl_intra = w_intra[..., -1, :]
    C_new = wl_inter[..., None, None] * C + jnp.einsum('bhs,bhsv,bhsk->bhvk', wl_intra, v, k)
    n_new = wl_inter[..., None] * n + jnp.einsum('bhs,bhsk->bhk', wl_intra, k)
    return (C_new, n_new, m_t[..., -1]), h


def mlstm_scan(q, k, v, ig, lf, state, chunk):
    B, H, T, dh = q.shape
    nc = T // chunk

    def to_chunks(a):
        a = a.reshape(a.shape[:2] + (nc, chunk) + a.shape[3:])
        return jnp.moveaxis(a, 2, 0)

    xs = (to_chunks(q), to_chunks(k), to_chunks(v), to_chunks(ig), to_chunks(lf))
    state, h = lax.scan(lambda s, c: mlstm_chunk(s, *c), state, xs)
    h = jnp.moveaxis(h, 0, 2).reshape(B, H, T, dh)
    return h, state


def index_select(iq, iw, ik, qpos, topk):
    dots = jnp.einsum('bthd,bsd->bths', iq.astype(F32), ik.astype(F32))
    score = jnp.einsum('bth,bths->bts', iw.astype(F32), jax.nn.relu(dots))
    kpos = jnp.arange(ik.shape[1])
    score = jnp.where(kpos[None, None, :] <= qpos[None, :, None], score, -jnp.inf)
    _, idx = lax.top_k(score, topk)
    valid = idx <= qpos[None, :, None]
    return idx, valid


def sparse_attend(q, ksel, vsel, valid):
    B, Tq = q.shape[:2]
    qg = q.reshape(B, Tq, S_KV_HEADS, S_REP, S_DH)
    s = jnp.einsum('btnrd,btknd->btnrk', qg, ksel).astype(F32) * (S_DH ** -0.5)
    s = jnp.where(valid[:, :, None, None, :], s, -jnp.inf)
    p = jax.nn.softmax(s, axis=-1).astype(vsel.dtype)
    o = jnp.einsum('btnrk,btknd->btnrd', p, vsel)
    return o.reshape(B, Tq, S_WIDTH)


def take_rows(rows, ids):
    return jax.vmap(lambda r, i: r[i])(rows, ids)


def dsa_prompt(q, kv, iq, iw, ik):
    B, T = q.shape[:2]
    topk = min(TOPK_MAX, T // 4)

    def block(i):
        t0 = i * Q_BLOCK
        sl = lambda a: lax.dynamic_slice_in_dim(a, t0, Q_BLOCK, axis=1)
        qpos = t0 + jnp.arange(Q_BLOCK)
        idx, valid = index_select(sl(iq), sl(iw), ik, qpos, topk)
        kv_sel = take_rows(kv, idx)
        return sparse_attend(sl(q), kv_sel[..., 0, :, :], kv_sel[..., 1, :, :], valid)

    o = lax.map(block, jnp.arange(T // Q_BLOCK))
    return jnp.moveaxis(o, 0, 1).reshape(B, T, S_WIDTH)


def dsa_sample(q, kv_new, iq, iw, ik_new, pool_kv, pool_ik, page_table):
    DB, DS = q.shape[:2]
    n_rows = (PAST_LEN // PAGE_SIZE) * PAGE_SIZE
    ik_past = pool_ik[page_table].reshape(DB, n_rows, IDX_DH)
    ik = jnp.concatenate([ik_past.astype(ik_new.dtype), ik_new], axis=1)
    qpos = PAST_LEN + jnp.arange(DS)
    topk = min(TOPK_MAX, (PAST_LEN + DS) // 4)
    idx, valid = index_select(iq, iw, ik, qpos, topk)
    is_new = idx >= PAST_LEN
    pidx = jnp.minimum(idx, PAST_LEN - 1)
    phys = jax.vmap(lambda pt, lp: pt[lp])(page_table, pidx // PAGE_SIZE)
    kv_past = pool_kv[phys, pidx % PAGE_SIZE]
    kv_cur = take_rows(kv_new, jnp.clip(idx - PAST_LEN, 0, DS - 1))
    kv_sel = jnp.where(is_new[..., None, None, None], kv_cur, kv_past.astype(kv_cur.dtype))
    return sparse_attend(q, kv_sel[..., 0, :, :], kv_sel[..., 1, :, :], valid)


def diff_attend(q, k, v, qpos, lam):
    s = jnp.einsum('bthcd,bshcd->bhcts', q, k).astype(F32) * (DF_DH ** -0.5)
    kpos = jnp.arange(k.shape[1])
    s = jnp.where(kpos[None, :] <= qpos[:, None], s, -jnp.inf)
    p = jax.nn.softmax(s, axis=-1)
    a = p[:, :, 0] - lam * p[:, :, 1]
    return jnp.einsum('bhts,bshe->bthe', a.astype(v.dtype), v)


def diff_prompt(q, k, v, lam):
    B, T = q.shape[:2]

    def block(i):
        t0 = i * Q_BLOCK
        qb = lax.dynamic_slice_in_dim(q, t0, Q_BLOCK, axis=1)
        return diff_attend(qb, k, v, t0 + jnp.arange(Q_BLOCK), lam)

    o = lax.map(block, jnp.arange(T // Q_BLOCK))
    return jnp.moveaxis(o, 0, 1).reshape(B, T, DF_HEADS, DF_VDH)


def diff_sample(q, k_new, v_new, pool_k, pool_v, page_table, lam):
    DB, DS = q.shape[:2]
    n_rows = (PAST_LEN // PAGE_SIZE) * PAGE_SIZE
    k_past = pool_k[page_table].reshape((DB, n_rows) + pool_k.shape[2:])
    v_past = pool_v[page_table].reshape((DB, n_rows) + pool_v.shape[2:])
    k = jnp.concatenate([k_past.astype(k_new.dtype), k_new], axis=1)
    v = jnp.concatenate([v_past.astype(v_new.dtype), v_new], axis=1)
    return diff_attend(q, k, v, PAST_LEN + jnp.arange(DS), lam)


def diff_lambda_value(lp, layer):
    lp = lp.astype(F32)
    lam_init = 0.8 - 0.6 * math.exp(-0.3 * layer)
    lam = jnp.exp(jnp.sum(lp[0] * lp[1])) - jnp.exp(jnp.sum(lp[2] * lp[3])) + lam_init
    return lam, lam_init


def mixer_inputs(x, l, pos, conv_buf, p):
    B, T = x.shape[:2]
    z = rmsnorm(x, p['norm_mix'][l]) @ p['w_in'][l]
    c = split_columns(z)
    qk, new_buf = causal_conv(jnp.concatenate([c['m_q'], c['m_k']], axis=-1), conv_buf,
                              p['mlstm_conv_w'][l], p['mlstm_conv_b'][l])
    qk = jax.nn.silu(qk).astype(F32)
    heads = lambda a: a.reshape(B, T, M_HEADS, M_DH).transpose(0, 2, 1, 3)
    gb = p['mlstm_gate_b'][l].astype(F32)
    return dict(
        mq=heads(qk[..., :M_WIDTH]),
        mk=heads(qk[..., M_WIDTH:]) * (M_DH ** -0.5),
        mv=heads(c['m_v'].astype(F32)),
        ig=(c['m_i'].astype(F32) + gb[0]).transpose(0, 2, 1),
        lf=jax.nn.log_sigmoid(c['m_f'].astype(F32) + gb[1]).transpose(0, 2, 1),
        mo=jax.nn.sigmoid(c['m_o']),
        conv_buf=new_buf,
        sq=rope(c['s_q'].reshape(B, T, S_HEADS, S_DH), pos),
        skv=jnp.stack([rope(c['s_k'].reshape(B, T, S_KV_HEADS, S_DH), pos),
                       c['s_v'].reshape(B, T, S_KV_HEADS, S_DH)], axis=2),
        iq=rope(c['ix_q'].reshape(B, T, IDX_HEADS, IDX_DH), pos),
        iw=c['ix_w'],
        ik=rope(c['ix_k'], pos),
        dq=rope(c['d_q'].reshape(B, T, DF_HEADS, 2, DF_DH), pos),
        dk=rope(c['d_k'].reshape(B, T, DF_HEADS, 2, DF_DH), pos),
        dv=c['d_v'].reshape(B, T, DF_HEADS, DF_VDH),
    )


def mix_and_ffn(x, l, h_m, o_m, s_out, d_out, lam_init, p):
    B, T = x.shape[:2]
    hm = rmsnorm(jnp.moveaxis(h_m, 1, 2), p['mlstm_norm'][l].reshape(M_HEADS, M_DH))
    hm = hm.reshape(B, T, M_WIDTH).astype(x.dtype) * o_m
    hd = (rmsnorm(d_out, p['diff_norm'][l]) * (1.0 - lam_init)).reshape(B, T, DF_WIDTH)
    mix = jnp.concatenate([hm, s_out.astype(x.dtype), hd.astype(x.dtype)], axis=-1)
    x = x + mix @ p['w_out'][l]
    u = jax.nn.relu(rmsnorm(x, p['norm_ffn'][l]) @ p['w_up'][l])
    return x + (u * u) @ p['w_down'][l]


def setup_inputs(seed: int = 0) -> dict:
    key = jax.random.key(seed)
    ks = jax.random.split(key, 26)
    n_pages = PAST_LEN // PAGE_SIZE
    used = DEC_BATCH * n_pages
    n_pool = used + max(1, used // 4)
    nrm = lambda k, shape, s=1.0: s * jax.random.normal(k, shape, F32)
    ig_b = nrm(ks[15], (DEPTH, 1, M_HEADS), 0.1)
    fg_b = 3.0 + 3.0 * jax.random.uniform(ks[16], (DEPTH, 1, M_HEADS), F32)
    return {
        'x_prompt': nrm(ks[0], (BATCH, SEQ, D_MODEL)),
        'x_sample': nrm(ks[1], (DEC_BATCH, DEC_SEQ, D_MODEL)),
        'cache_dsa_kv': nrm(ks[2], (DEPTH, n_pool, PAGE_SIZE, 2, S_KV_HEADS, S_DH)),
        'cache_idx_k': nrm(ks[3], (DEPTH, n_pool, PAGE_SIZE, IDX_DH)),
        'cache_diff_k': nrm(ks[4], (DEPTH, n_pool, PAGE_SIZE, DF_HEADS, 2, DF_DH)),
        'cache_diff_v': nrm(ks[5], (DEPTH, n_pool, PAGE_SIZE, DF_HEADS, DF_VDH)),
        'state_mlstm_C': nrm(ks[6], (DEPTH, DEC_BATCH, M_HEADS, M_DH, M_DH), 0.5),
        'state_mlstm_n': nrm(ks[7], (DEPTH, DEC_BATCH, M_HEADS, M_DH), 0.5),
        'state_mlstm_m': nrm(ks[8], (DEPTH, DEC_BATCH, M_HEADS)),
        'state_mlstm_conv': nrm(ks[9], (DEPTH, DEC_BATCH, CONV_W - 1, 2 * M_WIDTH)),
        'page_table': jax.random.permutation(ks[10], n_pool)[:used].reshape(DEC_BATCH, n_pages).astype(jnp.int32),
        'norm_mix': 1.0 + nrm(ks[11], (DEPTH, D_MODEL), 0.05),
        'w_in': nrm(ks[12], (DEPTH, D_MODEL, IN_WIDTH), D_MODEL ** -0.5),
        'mlstm_conv_w': nrm(ks[13], (DEPTH, CONV_W, 2 * M_WIDTH), CONV_W ** -0.5),
        'mlstm_conv_b': nrm(ks[14], (DEPTH, 2 * M_WIDTH), 0.02),
        'mlstm_gate_b': jnp.concatenate([ig_b, fg_b], axis=1),
        'mlstm_norm': 1.0 + nrm(ks[17], (DEPTH, M_WIDTH), 0.05),
        'diff_lambda': nrm(ks[18], (DEPTH, 4, DF_DH), 0.1),
        'diff_norm': 1.0 + nrm(ks[19], (DEPTH, DF_VDH), 0.05),
        'w_out': nrm(ks[20], (DEPTH, MIX_WIDTH, D_MODEL), MIX_WIDTH ** -0.5),
        'norm_ffn': 1.0 + nrm(ks[21], (DEPTH, D_MODEL), 0.05),
        'w_up': nrm(ks[22], (DEPTH, D_MODEL, D_FF), D_MODEL ** -0.5),
        'w_down': nrm(ks[23], (DEPTH, D_FF, D_MODEL), D_FF ** -0.5),
        'norm_final': 1.0 + nrm(ks[24], (D_MODEL,), 0.05),
    }


def reference(x_prompt, x_sample, cache_dsa_kv, cache_idx_k, cache_diff_k, cache_diff_v,
              state_mlstm_C, state_mlstm_n, state_mlstm_m, state_mlstm_conv, page_table,
              norm_mix, w_in, mlstm_conv_w, mlstm_conv_b, mlstm_gate_b, mlstm_norm,
              diff_lambda, diff_norm, w_out, norm_ffn, w_up, w_down, norm_final):
    p = dict(norm_mix=norm_mix, w_in=w_in, mlstm_conv_w=mlstm_conv_w, mlstm_conv_b=mlstm_conv_b,
             mlstm_gate_b=mlstm_gate_b, mlstm_norm=mlstm_norm, diff_norm=diff_norm, w_out=w_out,
             norm_ffn=norm_ffn, w_up=w_up, w_down=w_down)
    B, T = x_prompt.shape[:2]
    DB, DS = x_sample.shape[:2]
    pos_p = jnp.arange(T)
    pos_s = PAST_LEN + jnp.arange(DS)
    xp, xs = x_prompt, x_sample
    s_dtypes = (cache_dsa_kv.dtype, cache_idx_k.dtype, cache_diff_k.dtype, cache_diff_v.dtype,
                state_mlstm_C.dtype, state_mlstm_n.dtype, state_mlstm_m.dtype, state_mlstm_conv.dtype)
    rec_p = [[] for _ in range(8)]
    rec_s = [[] for _ in range(8)]
    for l in range(DEPTH):
        lam, lam_init = diff_lambda_value(diff_lambda[l], l)
        f = mixer_inputs(xp, l, pos_p, jnp.zeros((B, CONV_W - 1, 2 * M_WIDTH), xp.dtype), p)
        st0 = (jnp.zeros((B, M_HEADS, M_DH, M_DH), F32), jnp.zeros((B, M_HEADS, M_DH), F32),
               jnp.zeros((B, M_HEADS), F32))
        h_m, (C, n, m) = mlstm_scan(f['mq'], f['mk'], f['mv'], f['ig'], f['lf'], st0, M_CHUNK)
        s_out = dsa_prompt(f['sq'], f['skv'], f['iq'], f['iw'], f['ik'])
        d_out = diff_prompt(f['dq'], f['dk'], f['dv'], lam)
        xp = mix_and_ffn(xp, l, h_m, f['mo'], s_out, d_out, lam_init, p)
        for lst, a in zip(rec_p, (f['skv'], f['ik'], f['dk'], f['dv'], C, n, m, f['conv_buf'])):
            lst.append(a.astype(x_prompt.dtype))
        g = mixer_inputs(xs, l, pos_s, state_mlstm_conv[l], p)
        st = (state_mlstm_C[l].astype(F32), state_mlstm_n[l].astype(F32), state_mlstm_m[l].astype(F32))
        h_m, (C, n, m) = mlstm_scan(g['mq'], g['mk'], g['mv'], g['ig'], g['lf'], st, DS)
        s_out = dsa_sample(g['sq'], g['skv'], g['iq'], g['iw'], g['ik'],
                           cache_dsa_kv[l], cache_idx_k[l], page_table)
        d_out = diff_sample(g['dq'], g['dk'], g['dv'], cache_diff_k[l], cache_diff_v[l], page_table, lam)
        xs = mix_and_ffn(xs, l, h_m, g['mo'], s_out, d_out, lam_init, p)
        for lst, a, dt in zip(rec_s, (g['skv'], g['ik'], g['dk'], g['dv'], C, n, m, g['conv_buf']), s_dtypes):
            lst.append(a.astype(dt))
    y_prompt = rmsnorm(xp, norm_final)
    y_sample = rmsnorm(xs, norm_final)
    kv_p, ik_p, dk_p, dv_p, C_p, n_p, m_p, conv_p = [jnp.stack(a) for a in rec_p]
    kv_s, ik_s, dk_s, dv_s, C_s, n_s, m_s, conv_s = [jnp.stack(a) for a in rec_s]
    return (y_prompt, y_sample, kv_p, ik_p, dk_p, dv_p, C_p, n_p, m_p, conv_p,
            kv_s, ik_s, dk_s, dv_s, C_s, n_s, m_s, conv_s)
```

```python
import functools
import math

import jax
import jax.numpy as jnp
from jax import lax
from jax.experimental import pallas as pl
from jax.experimental.pallas import tpu as pltpu

F32 = jnp.float32
BF16 = jnp.bfloat16
I32 = jnp.int32

M_HEADS, M_DH = 4, 64
M_WIDTH = M_HEADS * M_DH
CONV_W = 4
S_HEADS, S_KV_HEADS, S_DH = 8, 2, 64
S_REP = S_HEADS // S_KV_HEADS
S_WIDTH = S_HEADS * S_DH
IDX_HEADS, IDX_DH = 8, 64
TOPK_MAX = 256
DF_HEADS, DF_DH = 4, 32
DF_VDH = 2 * DF_DH
DF_WIDTH = DF_HEADS * DF_VDH
ROPE_THETA = 10000.0
EPS = 1e-6

LANES = 128
VMEM_LIMIT = 56 * 1024 * 1024
INT_MIN = -(2 ** 31)
NEG_BIG = -1e30

C_UQK, C_MV, C_MO, C_SQ, C_SKV, C_IQ, C_DQ, C_DK, C_DV, C_MISC, C_END = (
    0, 512, 768, 1024, 1536, 1792, 2304, 2560, 2816, 3072, 3200)
G_IK, G_IG, G_LF, G_IW = 0, 64, 68, 72


def _cparams(sem):
    return pltpu.CompilerParams(dimension_semantics=sem, vmem_limit_bytes=VMEM_LIMIT)


def _swap_halves(zb, half, first):
    return jnp.where(first, pltpu.roll(zb, LANES - half, 1), pltpu.roll(zb, half, 1))


def _inproj_kernel(x_ref, g_ref, w_ref, gb_ref, c64_ref, s64_ref, c32_ref, s32_ref,
                   uqk_ref, mv_ref, mo_ref, sq_ref, skv_ref, iq_ref, dq_ref, dk_ref, dv_ref,
                   ik_ref, gates_ref):
    x = x_ref[...]
    ms = jnp.mean(x * x, axis=-1, keepdims=True)
    xn = (x * lax.rsqrt(ms + EPS) * g_ref[...]).astype(BF16)
    tm = x.shape[0]
    lane = lax.broadcasted_iota(I32, (tm, LANES), 1)
    first64 = (lane % 64) < 32
    first32 = (lane % 32) < 16
    c64, s64 = c64_ref[...], s64_ref[...]
    c32, s32 = c32_ref[...], s32_ref[...]

    def proj(a, b):
        return jnp.dot(xn, w_ref[:, a:b], preferred_element_type=F32)

    def rope_store(out_ref, z, nblk, c, s, half, first):
        for b in range(nblk):
            zb = z[:, b * LANES:(b + 1) * LANES]
            out_ref[:, b * LANES:(b + 1) * LANES] = zb * c + _swap_halves(zb, half, first) * s

    uqk_ref[...] = proj(C_UQK, C_MV)
    mv_ref[...] = proj(C_MV, C_MO)
    mo_ref[...] = jax.nn.sigmoid(proj(C_MO, C_SQ))
    rope_store(sq_ref, proj(C_SQ, C_SKV), 4, c64, s64, 32, first64)
    zkv = proj(C_SKV, C_IQ)
    rope_store(skv_ref, zkv, 1, c64, s64, 32, first64)
    skv_ref[:, LANES:] = zkv[:, LANES:]
    rope_store(iq_ref, proj(C_IQ, C_DQ), 4, c64, s64, 32, first64)
    rope_store(dq_ref, proj(C_DQ, C_DK), 2, c32, s32, 16, first32)
    rope_store(dk_ref, proj(C_DK, C_DV), 2, c32, s32, 16, first32)
    dv_ref[...] = proj(C_DV, C_MISC)
    zm = proj(C_MISC, C_END) + gb_ref[...]
    roped = zm * c64 + _swap_halves(zm, 32, first64) * s64
    logsig = jnp.minimum(zm, 0.0) - jnp.log1p(jnp.exp(-jnp.abs(zm)))
    is_lf = (lane >= G_LF) & (lane < G_IW)
    gates = jnp.where(lane < G_IG, roped, jnp.where(is_lf, logsig, zm))
    gates_ref[...] = gates
    ik_ref[...] = gates[:, :IDX_DH]


def _inproj(x2d, g, w_perm, gbias, tabs, tm):
    n, d = x2d.shape
    nt = tabs[0].shape[0] // tm
    row = lambda w: pl.BlockSpec((tm, w), lambda i: (i, 0))
    tab = pl.BlockSpec((tm, LANES), lambda i: (i % nt, 0))
    widths = (512, 256, 256, 512, 256, 512, 256, 256, 256, IDX_DH, LANES)
    return pl.pallas_call(
        _inproj_kernel,
        grid=(n // tm,),
        in_specs=[row(d), pl.BlockSpec((1, d), lambda i: (0, 0)),
                  pl.BlockSpec((d, C_END), lambda i: (0, 0)),
                  pl.BlockSpec((1, LANES), lambda i: (0, 0)), tab, tab, tab, tab],
        out_specs=[row(w) for w in widths],
        out_shape=[jax.ShapeDtypeStruct((n, w), F32) for w in widths],
        compiler_params=_cparams(("parallel",)),
        name="inproj",
    )(x2d, g, w_perm, gbias, *tabs)


def _permute_w_in(w):
    d = w.shape[0]
    cols = [w[:, 0:1024], w[:, 1032:1544], w[:, 1544:1800], w[:, 1800:2312], w[:, 2384:3152],
            w[:, 2312:2376], w[:, 1024:1032], w[:, 2376:2384],
            jnp.zeros((d, C_END - 3152), w.dtype)]
    return jnp.concatenate(cols, axis=1).astype(BF16)


def _rope_tables(pos, reps):
    out = []
    for dh in (64, 32):
        inv = ROPE_THETA ** (-jnp.arange(0, dh, 2, dtype=F32) / dh)
        ang = pos.astype(F32)[:, None] * inv[None, :]
        cos, sin = jnp.cos(ang), jnp.sin(ang)
        c = jnp.tile(jnp.concatenate([cos, cos], axis=1), (reps, LANES // dh))
        s = jnp.tile(jnp.concatenate([-sin, sin], axis=1), (reps, LANES // dh))
        out += [c, s]
    return tuple(out)


def _mlstm_kernel(uqk_ref, mv_ref, mo_ref, gc_ref, gr_ref, cw_ref, cb_ref, nrm_ref,
                  cbuf_ref, c0_ref, n0_ref, m0_ref,
                  hm_ref, c_out_ref, n_out_ref, m_out_ref,
                  ext_scr, c_scr, n_scr, m_scr, *, L):
    @pl.when(pl.program_id(1) == 0)
    def _():
        ext_scr[5:8, :] = cbuf_ref[0]
        c_scr[...] = c0_ref[0]
        n_scr[...] = n0_ref[0]
        m_scr[...] = m0_ref[0]

    u = uqk_ref[0]
    ext_scr[8:8 + L, :] = u
    y = (cb_ref[...] + ext_scr[5:5 + L, :] * cw_ref[0:1, :] + ext_scr[6:6 + L, :] * cw_ref[1:2, :]
         + ext_scr[7:7 + L, :] * cw_ref[2:3, :] + u * cw_ref[3:4, :])
    ext_scr[5:8, :] = ext_scr[L + 5:L + 8, :]
    qk = y * jax.nn.sigmoid(y)
    q_all = qk[:, :M_WIDTH]
    k_all = qk[:, M_WIDTH:] * (M_DH ** -0.5)
    v_all = mv_ref[0]
    o_all = mo_ref[0]
    gc = gc_ref[0]
    gr = gr_ref[0]

    row = lax.broadcasted_iota(I32, (L, L), 0)
    col = lax.broadcasted_iota(I32, (L, L), 1)
    tril = row >= col
    tri = tril.astype(F32)
    bc_all = jnp.dot(tri, gc, precision=lax.Precision.HIGHEST, preferred_element_type=F32)
    br_all = lax.dot_general(gr, tri, (((1,), (1,)), ((), ())), precision=lax.Precision.HIGHEST,
                             preferred_element_type=F32)

    for h in range(M_HEADS):
        sl = slice(h * M_DH, (h + 1) * M_DH)
        qh, kh, vh = q_all[:, sl], k_all[:, sl], v_all[:, sl]
        b_col = bc_all[:, G_LF + h:G_LF + h + 1]
        ig_col = gc[:, G_IG + h:G_IG + h + 1]
        b_row = br_all[M_HEADS + h:M_HEADS + h + 1, :]
        ig_row = gr[h:h + 1, :]
        m_prev = m_scr[h]
        c_prev = c_scr[h]
        n_prev = n_scr[h]

        a_col = b_col + m_prev
        dmat = jnp.where(tril, b_col - b_row + ig_row, -jnp.inf)
        m_t = jnp.maximum(a_col, jnp.max(dmat, axis=1, keepdims=True))
        w_inter = jnp.exp(a_col - m_t)
        w_intra = jnp.exp(dmat - m_t)
        qb, kb = qh.astype(BF16), kh.astype(BF16)
        s = lax.dot_general(qb, kb, (((1,), (1,)), ((), ())), preferred_element_type=F32)
        qkw = s * w_intra
        inter = lax.dot_general(qb, c_prev.astype(BF16), (((1,), (1,)), ((), ())),
                                preferred_element_type=F32)
        num = jnp.dot(qkw.astype(BF16), vh.astype(BF16), preferred_element_type=F32) + w_inter * inter
        den = (jnp.sum(qkw, axis=1, keepdims=True)
               + w_inter * jnp.sum(qh * n_prev, axis=1, keepdims=True))
        hh = num / jnp.maximum(jnp.abs(den), jnp.exp(-m_t))

        m_last = m_t[L - 1:L, :]
        wl_inter = w_inter[L - 1:L, :]
        wl_col = jnp.exp(b_col[L - 1:L, :] - b_col + ig_col - m_last)
        vw = (vh * wl_col).astype(BF16)
        c_new = wl_inter * c_prev + lax.dot_general(vw, kb, (((0,), (0,)), ((), ())),
                                                    preferred_element_type=F32)
        n_new = wl_inter * n_prev + jnp.sum(kh * wl_col, axis=0, keepdims=True)
        c_scr[h] = c_new
        n_scr[h] = n_new
        m_scr[h] = m_last
        c_out_ref[0, h] = c_new
        n_out_ref[0, h] = n_new
        m_out_ref[0, h] = m_last

        hn = hh * lax.rsqrt(jnp.mean(hh * hh, axis=1, keepdims=True) + EPS) * nrm_ref[:, sl]
        hm_ref[0, :, sl] = hn * o_all[:, sl]


def _mlstm(uqk, mv, mo, gates, gates_t, conv_w, conv_b, nrm, cbuf, c0, n0, m0, L):
    b, t, _ = uqk.shape
    tok = lambda w: pl.BlockSpec((1, L, w), lambda i, c: (i, c, 0))
    const2 = lambda a: pl.BlockSpec(a.shape, lambda i, c: (0, 0))
    per_b = lambda a: pl.BlockSpec((1,) + a.shape[1:], lambda i, c: (i,) + (0,) * (a.ndim - 1))
    out_shape = [jax.ShapeDtypeStruct((b, t, M_WIDTH), F32),
                 jax.ShapeDtypeStruct((b, M_HEADS, M_DH, M_DH), F32),
                 jax.ShapeDtypeStruct((b, M_HEADS, 1, M_DH), F32),
                 jax.ShapeDtypeStruct((b, M_HEADS, 1, 1), F32)]
    return pl.pallas_call(
        functools.partial(_mlstm_kernel, L=L),
        grid=(b, t // L),
        in_specs=[tok(2 * M_WIDTH), tok(M_WIDTH), tok(M_WIDTH), tok(LANES),
                  pl.BlockSpec((1, 16, L), lambda i, c: (i, 0, c)),
                  const2(conv_w), const2(conv_b), const2(nrm),
                  per_b(cbuf), per_b(c0), per_b(n0), per_b(m0)],
        out_specs=[tok(M_WIDTH)] + [pl.BlockSpec((1,) + s.shape[1:], lambda i, c: (i, 0, 0, 0))
                                    for s in out_shape[1:]],
        out_shape=out_shape,
        scratch_shapes=[pltpu.VMEM((L + 8, 2 * M_WIDTH), F32),
                        pltpu.VMEM((M_HEADS, M_DH, M_DH), F32),
                        pltpu.VMEM((M_HEADS, 1, M_DH), F32),
                        pltpu.VMEM((M_HEADS, 1, 1), F32)],
        compiler_params=_cparams(("parallel", "arbitrary")),
        name="mlstm",
    )(uqk, mv, mo, gates, gates_t, conv_w, conv_b, nrm, cbuf, c0, n0, m0)


def _sort_key(score):
    bits = pltpu.bitcast(score + 0.0, I32)
    return jnp.where(bits >= 0, bits, bits ^ jnp.int32(0x7FFFFFFF))


def _kth_largest(count_ge, rows, k):
    def body(it, ans):
        bit = jnp.left_shift(jnp.int32(1), 31 - it)
        cand = ans | bit
        cnt = count_ge(cand ^ jnp.int32(INT_MIN))
        return jnp.where(cnt >= k, cand, ans)
    ans = lax.fori_loop(0, 32, body, jnp.zeros((rows, 1), I32))
    return ans ^ jnp.int32(INT_MIN)


def _index_cut(count_eq_below, rows, need, nbits):
    def body(it, p):
        t = p | jnp.left_shift(jnp.int32(1), nbits - 1 - it)
        return jnp.where(count_eq_below(t) < need, t, p)
    return lax.fori_loop(0, nbits, body, jnp.zeros((rows, 1), I32))


def _dsa_prompt_kernel(sq_ref, iq_ref, g_ref, ik_ref, kv_ref, o_ref,
                       key_scr, m_scr, l_scr, acc_scr, *, tq, topk, nbits):
    i = pl.program_id(1)
    nblk = i + 1
    tk = tq
    row = lax.broadcasted_iota(I32, (tq, tk), 0)
    col = lax.broadcasted_iota(I32, (tq, tk), 1)
    causal = row >= col

    iq = iq_ref[0].astype(BF16)
    g = g_ref[0]
    half = IDX_HEADS // 2
    iq_st = [jnp.concatenate([iq[:, (gi * half + h) * IDX_DH:(gi * half + h + 1) * IDX_DH]
                              for h in range(half)], axis=0) for gi in range(2)]
    iw_st = [jnp.concatenate([g[:, G_IW + gi * half + h:G_IW + gi * half + h + 1]
                              for h in range(half)], axis=0) for gi in range(2)]

    def score_block(j, masked):
        ikb = ik_ref[0, pl.ds(pl.multiple_of(j * tk, tk), tk), :].astype(BF16)
        sc = jnp.zeros((tq, tk), F32)
        for gi in range(2):
            d = lax.dot_general(iq_st[gi], ikb, (((1,), (1,)), ((), ())), preferred_element_type=F32)
            r = jnp.maximum(d, 0.0) * iw_st[gi]
            for h in range(half):
                sc = sc + r[h * tq:(h + 1) * tq, :]
        key = _sort_key(sc)
        if masked:
            key = jnp.where(causal, key, jnp.int32(INT_MIN))
        key_scr[:, pl.ds(pl.multiple_of(j * tk, tk), tk)] = key

    lax.fori_loop(0, nblk - 1, lambda j, c: (score_block(j, False), c)[1], 0)
    score_block(nblk - 1, True)

    def lane_fold(x):
        out = x[:, :LANES]
        for b in range(1, tk // LANES):
            out = out + x[:, b * LANES:(b + 1) * LANES]
        return out

    def count(pred):
        def body(j, acc):
            kb = key_scr[:, pl.ds(pl.multiple_of(j * tk, tk), tk)]
            return acc + lane_fold(jnp.where(pred(kb, j), 1, 0).astype(I32))
        acc = lax.fori_loop(0, nblk, body, jnp.zeros((tq, LANES), I32))
        return jnp.sum(acc, axis=1, keepdims=True)

    thr = _kth_largest(lambda v: count(lambda kb, j: kb >= v), tq, topk)
    n_gt = count(lambda kb, j: kb > thr)
    need = topk - n_gt
    cut = _index_cut(lambda t: count(lambda kb, j: (kb == thr) & (j * tk + col < t)), tq, need, nbits)

    sq = (sq_ref[0] * (S_DH ** -0.5)).astype(BF16)
    q_st = [jnp.concatenate([sq[:, (n * S_REP + r) * S_DH:(n * S_REP + r + 1) * S_DH]
                             for r in range(S_REP)], axis=0) for n in range(S_KV_HEADS)]
    m_scr[...] = jnp.full(m_scr.shape, -jnp.inf, F32)
    l_scr[...] = jnp.zeros(l_scr.shape, F32)
    acc_scr[...] = jnp.zeros(acc_scr.shape, F32)

    def attend_block(j, masked):
        off = pl.multiple_of(j * tk, tk)
        kb = key_scr[:, pl.ds(off, tk)]
        sel = (kb > thr) | ((kb == thr) & (j * tk + col <= cut))
        if masked:
            sel = sel & causal
        kvb = kv_ref[0, pl.ds(off, tk), :].astype(BF16)
        for n in range(S_KV_HEADS):
            kn = kvb[:, n * S_DH:(n + 1) * S_DH]
            vn = kvb[:, (S_KV_HEADS + n) * S_DH:(S_KV_HEADS + n + 1) * S_DH]
            s = lax.dot_general(q_st[n], kn, (((1,), (1,)), ((), ())), preferred_element_type=F32)
            s = jnp.where(sel[None], s.reshape(S_REP, tq, tk), -jnp.inf).reshape(S_REP * tq, tk)
            m_old = m_scr[n]
            m_new = jnp.maximum(m_old, jnp.max(s, axis=1, keepdims=True))
            m_safe = jnp.where(m_new == -jnp.inf, 0.0, m_new)
            alpha = jnp.exp(m_old - m_safe)
            p = jnp.exp(s - m_safe)
            l_scr[n] = alpha * l_scr[n] + jnp.sum(p, axis=1, keepdims=True)
            acc_scr[n] = alpha * acc_scr[n] + jnp.dot(p.astype(BF16), vn, preferred_element_type=F32)
            m_scr[n] = m_new

    lax.fori_loop(0, nblk - 1, lambda j, c: (attend_block(j, False), c)[1], 0)
    attend_block(nblk - 1, True)

    for n in range(S_KV_HEADS):
        o = acc_scr[n] / l_scr[n]
        for r in range(S_REP):
            h = n * S_REP + r
            o_ref[0, :, h * S_DH:(h + 1) * S_DH] = o[r * tq:(r + 1) * tq, :]


def _dsa_prompt(sq, iq, gates, ik, skv, tq):
    b, t, _ = sq.shape
    topk = min(TOPK_MAX, t // 4)
    nbits = max(1, (t - 1).bit_length())
    blk = lambda w: pl.BlockSpec((1, tq, w), lambda bi, i: (bi, i, 0))
    full = lambda w: pl.BlockSpec((1, t, w), lambda bi, i: (bi, 0, 0))
    return pl.pallas_call(
        functools.partial(_dsa_prompt_kernel, tq=tq, topk=topk, nbits=nbits),
        grid=(b, t // tq),
        in_specs=[blk(S_WIDTH), blk(IDX_HEADS * IDX_DH), blk(LANES), full(IDX_DH), full(2 * S_KV_HEADS * S_DH)],
        out_specs=blk(S_WIDTH),
        out_shape=jax.ShapeDtypeStruct((b, t, S_WIDTH), F32),
        scratch_shapes=[pltpu.VMEM((tq, t), I32),
                        pltpu.VMEM((S_KV_HEADS, S_REP * tq, 1), F32),
                        pltpu.VMEM((S_KV_HEADS, S_REP * tq, 1), F32),
                        pltpu.VMEM((S_KV_HEADS, S_REP * tq, S_DH), F32)],
        compiler_params=_cparams(("parallel", "parallel")),
        name="dsa_prompt",
    )(sq, iq, gates, ik, skv)


def _diff_lambda(lp_ref, lam_init):
    lp = lp_ref[...]
    a = jnp.sum(lp[0:1, :] * lp[1:2, :], axis=1, keepdims=True)
    b = jnp.sum(lp[2:3, :] * lp[3:4, :], axis=1, keepdims=True)
    return jnp.exp(a) - jnp.exp(b) + lam_init


def _diff_finish(acc, l, lam, gn, lam_init):
    outs = []
    for h in range(DF_HEADS):
        o = acc[2 * h] / l[2 * h] - lam * (acc[2 * h + 1] / l[2 * h + 1])
        o = o * lax.rsqrt(jnp.mean(o * o, axis=1, keepdims=True) + EPS) * gn
        outs.append(o * (1.0 - lam_init))
    return outs


def _diff_prompt_kernel(lp_ref, gn_ref, q_ref, k_ref, v_ref, o_ref, m_scr, l_scr, acc_scr,
                        *, tq, lam_init):
    i = pl.program_id(1)
    tk = tq
    nmap = 2 * DF_HEADS
    row = lax.broadcasted_iota(I32, (tq, tk), 0)
    col = lax.broadcasted_iota(I32, (tq, tk), 1)
    causal = row >= col
    q = (q_ref[0] * (DF_DH ** -0.5)).astype(BF16)
    q_maps = [q[:, mi * DF_DH:(mi + 1) * DF_DH] for mi in range(nmap)]
    m_scr[...] = jnp.full(m_scr.shape, -jnp.inf, F32)
    l_scr[...] = jnp.zeros(l_scr.shape, F32)
    acc_scr[...] = jnp.zeros(acc_scr.shape, F32)

    def block(j, masked):
        off = pl.multiple_of(j * tk, tk)
        kb = k_ref[0, pl.ds(off, tk), :].astype(BF16)
        vb = v_ref[0, pl.ds(off, tk), :].astype(BF16)
        for mi in range(nmap):
            h = mi // 2
            s = lax.dot_general(q_maps[mi], kb[:, mi * DF_DH:(mi + 1) * DF_DH],
                                (((1,), (1,)), ((), ())), preferred_element_type=F32)
            if masked:
                s = jnp.where(causal, s, -jnp.inf)
            m_old = m_scr[mi]
            m_new = jnp.maximum(m_old, jnp.max(s, axis=1, keepdims=True))
            alpha = jnp.exp(m_old - m_new)
            p = jnp.exp(s - m_new)
            l_scr[mi] = alpha * l_scr[mi] + jnp.sum(p, axis=1, keepdims=True)
            acc_scr[mi] = alpha * acc_scr[mi] + jnp.dot(
                p.astype(BF16), vb[:, h * DF_VDH:(h + 1) * DF_VDH], preferred_element_type=F32)
            m_scr[mi] = m_new

    block(i, True)
    lax.fori_loop(0, i, lambda j, c: (block(j, False), c)[1], 0)

    lam = _diff_lambda(lp_ref, lam_init)
    outs = _diff_finish([acc_scr[mi] for mi in range(nmap)], [l_scr[mi] for mi in range(nmap)],
                        lam, gn_ref[...], lam_init)
    for h in range(DF_HEADS):
        o_ref[0, :, h * DF_VDH:(h + 1) * DF_VDH] = outs[h]


def _diff_prompt(lp, gn, dq, dk, dv, lam_init, tq):
    b, t, w = dq.shape
    blk = pl.BlockSpec((1, tq, w), lambda bi, i: (bi, i, 0))
    full = pl.BlockSpec((1, t, w), lambda bi, i: (bi, 0, 0))
    nmap = 2 * DF_HEADS
    return pl.pallas_call(
        functools.partial(_diff_prompt_kernel, tq=tq, lam_init=lam_init),
        grid=(b, t // tq),
        in_specs=[pl.BlockSpec(lp.shape, lambda bi, i: (0, 0)), pl.BlockSpec(gn.shape, lambda bi, i: (0, 0)),
                  blk, full, full],
        out_specs=blk,
        out_shape=jax.ShapeDtypeStruct((b, t, w), F32),
        scratch_shapes=[pltpu.VMEM((nmap, tq, 1), F32), pltpu.VMEM((nmap, tq, 1), F32),
                        pltpu.VMEM((nmap, tq, DF_VDH), F32)],
        compiler_params=_cparams(("parallel", "parallel")),
        name="diff_prompt",
    )(lp, gn, dq, dk, dv)


def _out_ffn_kernel(x_ref, hm_ref, so_ref, hd_ref, wo_ref, gf_ref, wu_ref, wd_ref, gl_ref, y_ref,
                    xn_scr, *, final_norm):
    c = pl.program_id(1)

    @pl.when(c == 0)
    def _():
        x1 = (x_ref[...]
              + jnp.dot(hm_ref[...].astype(BF16), wo_ref[0:M_WIDTH, :], preferred_element_type=F32)
              + jnp.dot(so_ref[...].astype(BF16), wo_ref[M_WIDTH:M_WIDTH + S_WIDTH, :],
                        preferred_element_type=F32)
              + jnp.dot(hd_ref[...].astype(BF16), wo_ref[M_WIDTH + S_WIDTH:, :], preferred_element_type=F32))
        xn_scr[...] = (x1 * lax.rsqrt(jnp.mean(x1 * x1, axis=-1, keepdims=True) + EPS)
                       * gf_ref[...]).astype(BF16)
        y_ref[...] = x1

    u = jnp.maximum(jnp.dot(xn_scr[...], wu_ref[...], preferred_element_type=F32), 0.0)
    y_ref[...] += jnp.dot((u * u).astype(BF16), wd_ref[...], preferred_element_type=F32)

    if final_norm:
        @pl.when(c == pl.num_programs(1) - 1)
        def _():
            y = y_ref[...]
            y_ref[...] = y * lax.rsqrt(jnp.mean(y * y, axis=-1, keepdims=True) + EPS) * gl_ref[...]


def _out_ffn(x2d, hm, so, hd, wo, gf, wu, wd, gl, final_norm, tm, tf=1024):
    n, d = x2d.shape
    row = lambda w: pl.BlockSpec((tm, w), lambda i, c: (i, 0))
    const = lambda a: pl.BlockSpec(a.shape, lambda i, c: (0, 0))
    return pl.pallas_call(
        functools.partial(_out_ffn_kernel, final_norm=final_norm),
        grid=(n // tm, wu.shape[1] // tf),
        in_specs=[row(d), row(M_WIDTH), row(S_WIDTH), row(DF_WIDTH), const(wo), const(gf),
                  pl.BlockSpec((d, tf), lambda i, c: (0, c)), pl.BlockSpec((tf, d), lambda i, c: (c, 0)),
                  const(gl)],
        out_specs=row(d),
        out_shape=jax.ShapeDtypeStruct((n, d), F32),
        scratch_shapes=[pltpu.VMEM((tm, d), BF16)],
        compiler_params=_cparams(("parallel", "arbitrary")),
        name="out_ffn",
    )(x2d, hm, so, hd, wo, gf, wu, wd, gl)


def _page_specs(n, page_rows, width, pages_per_step):
    return [pl.BlockSpec((1, page_rows, width),
                         functools.partial(lambda b, c, pt, p: (pt[b, c * pages_per_step + p], 0, 0), p=p))
            for p in range(n)]


def _seq_spec(shape):
    return pl.BlockSpec((1,) + shape, lambda b, c, pt: (b,) + (0,) * len(shape))


def _dsa_sample_score_kernel(pt_ref, iq_ref, iw_ref, ikn_ref, *rest, P, page, ds, topk, nbits):
    pages = rest[:P]
    keys_ref, tkeys_ref, thr_ref, cut_ref, key_scr = rest[P:]
    c = pl.program_id(1)
    nc = pl.num_programs(1)
    past = nc * P * page
    iq = iq_ref[0].astype(BF16)
    iw = iw_ref[0]

    def scores(kblock):
        d = lax.dot_general(iq, kblock.astype(BF16), (((1,), (1,)), ((), ())), preferred_element_type=F32)
        r = jnp.maximum(d, 0.0) * iw
        return jnp.sum(r.reshape(ds, IDX_HEADS, kblock.shape[0]), axis=1)

    pad_rows = jnp.full((8 - ds, page), INT_MIN, I32)
    for p in range(P):
        key = jnp.concatenate([_sort_key(scores(pages[p][0])), pad_rows], axis=0)
        keys_ref[0, :, p * page:(p + 1) * page] = key
        key_scr[:, pl.ds(pl.multiple_of((c * P + p) * page, page), page)] = key

    @pl.when(c == nc - 1)
    def _():
        trow = lax.broadcasted_iota(I32, (ds, page), 0)
        tcol = lax.broadcasted_iota(I32, (ds, page), 1)
        tkey = jnp.where(tcol <= trow, _sort_key(scores(ikn_ref[0])), jnp.int32(INT_MIN))
        tkey = jnp.concatenate([tkey, pad_rows], axis=0)
        tkeys_ref[0] = tkey
        key_scr[:, past:past + page] = tkey
        keys = key_scr[...]
        idx = lax.broadcasted_iota(I32, keys.shape, 1)

        def cnt(pred):
            return jnp.sum(jnp.where(pred, 1, 0).astype(I32), axis=1, keepdims=True)

        thr = _kth_largest(lambda v: cnt(keys >= v), 8, topk)
        need = topk - cnt(keys > thr)
        cut = _index_cut(lambda t: cnt((keys == thr) & (idx < t)), 8, need, nbits)
        thr_ref[0] = jnp.broadcast_to(thr, (8, LANES))
        cut_ref[0] = jnp.broadcast_to(cut, (8, LANES))


def _dsa_sample_score(pt, iq32, iw32, ik_new_pad, pool_ik, ds, P):
    db, n_pages = pt.shape
    page = pool_ik.shape[1]
    past = n_pages * page
    nc = n_pages // P
    topk = min(TOPK_MAX, (past + ds) // 4)
    nbits = max(1, (past + page - 1).bit_length())
    out_shape = [jax.ShapeDtypeStruct((db, 8, past), I32)] + [jax.ShapeDtypeStruct((db, 8, LANES), I32)] * 3
    grid_spec = pltpu.PrefetchScalarGridSpec(
        num_scalar_prefetch=1, grid=(db, nc),
        in_specs=[_seq_spec(iq32.shape[1:]), _seq_spec(iw32.shape[1:]), _seq_spec(ik_new_pad.shape[1:])]
        + _page_specs(P, page, IDX_DH, P),
        out_specs=[pl.BlockSpec((1, 8, P * page), lambda b, c, pt: (b, 0, c))] + [_seq_spec((8, LANES))] * 3,
        scratch_shapes=[pltpu.VMEM((8, past + page), I32)])
    return pl.pallas_call(
        functools.partial(_dsa_sample_score_kernel, P=P, page=page, ds=ds, topk=topk, nbits=nbits),
        grid_spec=grid_spec, out_shape=out_shape,
        compiler_params=_cparams(("parallel", "arbitrary")),
        name="dsa_sample_score",
    )(pt, iq32, iw32, ik_new_pad, *([pool_ik] * P))


def _softmax_step(s, m_scr, l_scr):
    m_old = m_scr[...]
    m_new = jnp.maximum(m_old, jnp.max(s, axis=1, keepdims=True))
    m_safe = jnp.where(m_new == -jnp.inf, 0.0, m_new)
    alpha = jnp.exp(m_old - m_safe)
    p = jnp.exp(s - m_safe)
    l_scr[...] = alpha * l_scr[...] + jnp.sum(p, axis=1, keepdims=True)
    m_scr[...] = m_new
    return alpha, p


def _dsa_sample_attn_kernel(pt_ref, q_ref, keys_ref, tkeys_ref, thr_ref, cut_ref, kvn_ref, *rest,
                            P, page, ds):
    pages = rest[:P]
    o_ref, m_scr, l_scr, acc_scr = rest[P:]
    c = pl.program_id(1)
    nc = pl.num_programs(1)
    past = nc * P * page
    rows = ds * S_HEADS
    kw = S_KV_HEADS * S_DH

    @pl.when(c == 0)
    def _():
        m_scr[...] = jnp.full(m_scr.shape, -jnp.inf, F32)
        l_scr[...] = jnp.zeros(l_scr.shape, F32)
        acc_scr[...] = jnp.zeros(acc_scr.shape, F32)

    q = q_ref[0] * (S_DH ** -0.5)
    rr = lax.broadcasted_iota(I32, (rows, kw), 0)
    ll = lax.broadcasted_iota(I32, (rows, kw), 1)
    own = (ll // S_DH) == ((rr % S_HEADS) // S_REP)
    qbd = jnp.where(own, jnp.concatenate([q] * S_KV_HEADS, axis=1), 0.0).astype(BF16)
    thr = thr_ref[0][:ds, 0:1]
    cut = cut_ref[0][:ds, 0:1]

    def expand(sel):
        return jnp.broadcast_to(sel[:, None, :], (ds, S_HEADS, sel.shape[1])).reshape(rows, sel.shape[1])

    def attend(kv_blocks, sel):
        s = jnp.concatenate([lax.dot_general(qbd, kb[:, :kw].astype(BF16), (((1,), (1,)), ((), ())),
                                             preferred_element_type=F32) for kb in kv_blocks], axis=1)
        s = jnp.where(expand(sel), s, -jnp.inf)
        alpha, p = _softmax_step(s, m_scr, l_scr)
        pv = jnp.zeros((rows, kw), F32)
        for bi, kb in enumerate(kv_blocks):
            pv = pv + jnp.dot(p[:, bi * page:(bi + 1) * page].astype(BF16), kb[:, kw:].astype(BF16),
                              preferred_element_type=F32)
        acc_scr[...] = alpha * acc_scr[...] + pv

    keys = keys_ref[0][:ds, :]
    col = lax.broadcasted_iota(I32, keys.shape, 1) + c * (P * page)
    attend([pg[0] for pg in pages], (keys > thr) | ((keys == thr) & (col <= cut)))

    @pl.when(c == nc - 1)
    def _():
        tkeys = tkeys_ref[0][:ds, :]
        trow = lax.broadcasted_iota(I32, tkeys.shape, 0)
        tcol = lax.broadcasted_iota(I32, tkeys.shape, 1)
        sel = ((tkeys > thr) | ((tkeys == thr) & (tcol + past <= cut))) & (tcol <= trow)
        attend([kvn_ref[0]], sel)
        o = acc_scr[...] / l_scr[...]
        first = ((lax.broadcasted_iota(I32, (rows, S_DH), 0) % S_HEADS) // S_REP) == 0
        o_ref[0] = jnp.where(first, o[:, :S_DH], o[:, S_DH:])


def _dsa_sample_attn(pt, q32, keys, tkeys, thr, cut, kv_new_pad, pool_kv, ds, P):
    db, n_pages = pt.shape
    page = pool_kv.shape[1]
    nc = n_pages // P
    rows = ds * S_HEADS
    kw = S_KV_HEADS * S_DH
    grid_spec = pltpu.PrefetchScalarGridSpec(
        num_scalar_prefetch=1, grid=(db, nc),
        in_specs=[_seq_spec(q32.shape[1:]), pl.BlockSpec((1, 8, P * page), lambda b, c, pt: (b, 0, c)),
                  _seq_spec((8, LANES)), _seq_spec((8, LANES)), _seq_spec((8, LANES)),
                  _seq_spec(kv_new_pad.shape[1:])] + _page_specs(P, page, 2 * kw, P),
        out_specs=_seq_spec((rows, S_DH)),
        scratch_shapes=[pltpu.VMEM((rows, 1), F32), pltpu.VMEM((rows, 1), F32), pltpu.VMEM((rows, kw), F32)])
    return pl.pallas_call(
        functools.partial(_dsa_sample_attn_kernel, P=P, page=page, ds=ds),
        grid_spec=grid_spec, out_shape=jax.ShapeDtypeStruct((db, rows, S_DH), F32),
        compiler_params=_cparams(("parallel", "arbitrary")),
        name="dsa_sample_attn",
    )(pt, q32, keys, tkeys, thr, cut, kv_new_pad, *([pool_kv] * P))


def _diff_sample_kernel(pt_ref, lp_ref, gn_ref, q_ref, kn_ref, vn_ref, *rest, P, page, ds, lam_init):
    kpages, vpages = rest[:P], rest[P:2 * P]
    o_ref, m_scr, l_scr, acc_scr = rest[2 * P:]
    c = pl.program_id(1)
    nc = pl.num_programs(1)
    nmap = 2 * DF_HEADS
    rows = ds * nmap
    half = rows // 2

    @pl.when(c == 0)
    def _():
        m_scr[...] = jnp.full(m_scr.shape, -jnp.inf, F32)
        l_scr[...] = jnp.zeros(l_scr.shape, F32)
        acc_scr[...] = jnp.zeros(acc_scr.shape, F32)

    q = q_ref[0] * (DF_DH ** -0.5)
    rr = lax.broadcasted_iota(I32, (rows, DF_WIDTH), 0)
    ll = lax.broadcasted_iota(I32, (rows, DF_WIDTH), 1)
    own = (ll // DF_DH) == (2 * (rr % DF_HEADS) + rr // half)
    qbd = jnp.where(own, jnp.concatenate([q] * nmap, axis=1), 0.0).astype(BF16)

    def attend(kblocks, vblocks, mask):
        s = jnp.concatenate([lax.dot_general(qbd, kb.astype(BF16), (((1,), (1,)), ((), ())),
                                             preferred_element_type=F32) for kb in kblocks], axis=1)
        if mask is not None:
            s = jnp.where(mask, s, -jnp.inf)
        alpha, p = _softmax_step(s, m_scr, l_scr)
        pv = jnp.zeros((rows, DF_WIDTH), F32)
        for bi, vb in enumerate(vblocks):
            pv = pv + jnp.dot(p[:, bi * page:(bi + 1) * page].astype(BF16), vb.astype(BF16),
                              preferred_element_type=F32)
        acc_scr[...] = alpha * acc_scr[...] + pv

    attend([pg[0] for pg in kpages], [pg[0] for pg in vpages], None)

    @pl.when(c == nc - 1)
    def _():
        trow = (lax.broadcasted_iota(I32, (rows, page), 0) % half) // DF_HEADS
        tcol = lax.broadcasted_iota(I32, (rows, page), 1)
        attend([kn_ref[0]], [vn_ref[0]], tcol <= trow)
        o = acc_scr[...] / l_scr[...]
        hrow = lax.broadcasted_iota(I32, (rows, DF_VDH), 0) % DF_HEADS
        oh = jnp.zeros((rows, DF_VDH), F32)
        for h in range(DF_HEADS):
            oh = jnp.where(hrow == h, o[:, h * DF_VDH:(h + 1) * DF_VDH], oh)
        lam = _diff_lambda(lp_ref, lam_init)
        d = oh[:half, :] - lam * oh[half:, :]
        d = d * lax.rsqrt(jnp.mean(d * d, axis=1, keepdims=True) + EPS) * gn_ref[...]
        o_ref[0] = d * (1.0 - lam_init)


def _diff_sample(pt, lp, gn, q32, k_new_pad, v_new_pad, pool_k, pool_v, lam_init, ds, P):
    db, n_pages = pt.shape
    page = pool_k.shape[1]
    nc = n_pages // P
    rows = ds * 2 * DF_HEADS
    const = lambda a: pl.BlockSpec(a.shape, lambda b, c, pt: (0, 0))
    grid_spec = pltpu.PrefetchScalarGridSpec(
        num_scalar_prefetch=1, grid=(db, nc),
        in_specs=[const(lp), const(gn), _seq_spec(q32.shape[1:]), _seq_spec(k_new_pad.shape[1:]),
                  _seq_spec(v_new_pad.shape[1:])]
        + _page_specs(P, page, DF_WIDTH, P) + _page_specs(P, page, DF_WIDTH, P),
        out_specs=_seq_spec((rows // 2, DF_VDH)),
        scratch_shapes=[pltpu.VMEM((rows, 1), F32), pltpu.VMEM((rows, 1), F32),
                        pltpu.VMEM((rows, DF_WIDTH), F32)])
    return pl.pallas_call(
        functools.partial(_diff_sample_kernel, P=P, page=page, ds=ds, lam_init=lam_init),
        grid_spec=grid_spec, out_shape=jax.ShapeDtypeStruct((db, rows // 2, DF_VDH), F32),
        compiler_params=_cparams(("parallel", "arbitrary")),
        name="diff_sample",
    )(pt, lp, gn, q32, k_new_pad, v_new_pad, *([pool_k] * P), *([pool_v] * P))


def _largest_tile(n, cap):
    t = cap
    while n % t:
        t //= 2
    return t


def _pad_time(a, to, value=0.0):
    return jnp.pad(a, ((0, 0), (0, to - a.shape[1]), (0, 0)), constant_values=value)


def kernel(x_prompt, x_sample, cache_dsa_kv, cache_idx_k, cache_diff_k, cache_diff_v, state_mlstm_C, state_mlstm_n, state_mlstm_m, state_mlstm_conv, page_table, norm_mix, w_in, mlstm_conv_w, mlstm_conv_b, mlstm_gate_b, mlstm_norm, diff_lambda, diff_norm, w_out, norm_ffn, w_up, w_down, norm_final):
    B, T, D = x_prompt.shape
    DB, DS, _ = x_sample.shape
    depth = w_in.shape[0]
    n_pool, page = cache_idx_k.shape[1:3]
    past = page_table.shape[1] * page
    P = _largest_tile(page_table.shape[1], 16)
    LS = LANES

    tm_p = _largest_tile(T, 512)
    tabs_p = _rope_tables(jnp.arange(T), 1)
    tabs_s = _rope_tables(past + jnp.arange(DS), DB)
    gate_pad = jnp.zeros((LANES,), F32).at[G_IG:G_IG + M_HEADS].set(NEG_BIG)

    xp = x_prompt.reshape(B * T, D)
    xs = x_sample.reshape(DB * DS, D)
    rec_p = [[] for _ in range(8)]
    rec_s = [[] for _ in range(8)]
    for l in range(depth):
        lam_init = 0.8 - 0.6 * math.exp(-0.3 * l)
        last = l == depth - 1
        w_perm = _permute_w_in(w_in[l])
        g_mix = norm_mix[l][None, :]
        gbias = jnp.zeros((1, LANES), F32).at[0, G_IG:G_IG + M_HEADS].set(mlstm_gate_b[l, 0]) \
            .at[0, G_LF:G_LF + M_HEADS].set(mlstm_gate_b[l, 1])
        conv_w, conv_b = mlstm_conv_w[l], mlstm_conv_b[l][None, :]
        nrm_m = mlstm_norm[l][None, :]
        lp, gn = diff_lambda[l], diff_norm[l][None, :]
        wo, wu, wd = w_out[l].astype(BF16), w_up[l].astype(BF16), w_down[l].astype(BF16)
        g_ffn, g_fin = norm_ffn[l][None, :], norm_final[None, :]

        (uqk, mv, mo, sq, skv, iq, dq, dk, dv, ik, gates) = [
            a.reshape(B, T, -1) for a in _inproj(xp, g_mix, w_perm, gbias, tabs_p, tm_p)]
        gates_t = jnp.swapaxes(gates[:, :, G_IG:G_IG + 16], 1, 2)
        hm, C, n, m = _mlstm(uqk, mv, mo, gates, gates_t, conv_w, conv_b, nrm_m,
                             jnp.zeros((B, CONV_W - 1, 2 * M_WIDTH), F32),
                             jnp.zeros((B, M_HEADS, M_DH, M_DH), F32), jnp.zeros((B, M_HEADS, 1, M_DH), F32),
                             jnp.zeros((B, M_HEADS, 1, 1), F32), _largest_tile(T, 256))
        s_out = _dsa_prompt(sq, iq, gates, ik, skv, _largest_tile(T, 256))
        hd = _diff_prompt(lp, gn, dq, dk, dv, lam_init, _largest_tile(T, 256))
        xp = _out_ffn(xp, hm.reshape(B * T, -1), s_out.reshape(B * T, -1), hd.reshape(B * T, -1),
                      wo, g_ffn, wu, wd, g_fin, last, tm_p)
        for lst, a in zip(rec_p, (skv.reshape(B, T, 2, S_KV_HEADS, S_DH), ik,
                                  dk.reshape(B, T, DF_HEADS, 2, DF_DH), dv.reshape(B, T, DF_HEADS, DF_VDH),
                                  C, n.reshape(B, M_HEADS, M_DH), m.reshape(B, M_HEADS),
                                  uqk[:, T - (CONV_W - 1):, :])):
            lst.append(a)

        (uqk, mv, mo, sq, skv, iq, dq, dk, dv, ik, gates) = [
            a.reshape(DB, DS, -1) for a in _inproj(xs, g_mix, w_perm, gbias, tabs_s, DB * DS)]
        gates_pd = jnp.concatenate([gates, jnp.broadcast_to(gate_pad, (DB, LS - DS, LANES))], axis=1)
        gates_t = jnp.swapaxes(gates_pd[:, :, G_IG:G_IG + 16], 1, 2)
        hm, C, n, m = _mlstm(_pad_time(uqk, LS), _pad_time(mv, LS), _pad_time(mo, LS), gates_pd, gates_t,
                             conv_w, conv_b, nrm_m, state_mlstm_conv[l], state_mlstm_C[l],
                             state_mlstm_n[l][:, :, None, :], state_mlstm_m[l][:, :, None, None], LS)
        hm = hm[:, :DS, :]
        keys, tkeys, thr, cut = _dsa_sample_score(
            page_table, iq.reshape(DB, DS * IDX_HEADS, IDX_DH),
            gates[:, :, G_IW:G_IW + IDX_HEADS].reshape(DB, DS * IDX_HEADS, 1),
            _pad_time(ik, page), cache_idx_k[l], DS, P)
        s_out = _dsa_sample_attn(page_table, sq.reshape(DB, DS * S_HEADS, S_DH), keys, tkeys, thr, cut,
                                 _pad_time(skv, page), cache_dsa_kv[l].reshape(n_pool, page, -1), DS, P)
        dq32 = dq.reshape(DB, DS, DF_HEADS, 2, DF_DH).transpose(0, 3, 1, 2, 4).reshape(DB, -1, DF_DH)
        hd = _diff_sample(page_table, lp, gn, dq32, _pad_time(dk, page), _pad_time(dv, page),
                          cache_diff_k[l].reshape(n_pool, page, -1), cache_diff_v[l].reshape(n_pool, page, -1),
                          lam_init, DS, P)
        xs = _out_ffn(xs, hm.reshape(DB * DS, -1), s_out.reshape(DB * DS, -1), hd.reshape(DB * DS, -1),
                      wo, g_ffn, wu, wd, g_fin, last, DB * DS)
        for lst, a in zip(rec_s, (skv.reshape(DB, DS, 2, S_KV_HEADS, S_DH), ik,
                                  dk.reshape(DB, DS, DF_HEADS, 2, DF_DH), dv.reshape(DB, DS, DF_HEADS, DF_VDH),
                                  C, n.reshape(DB, M_HEADS, M_DH), m.reshape(DB, M_HEADS),
                                  jnp.concatenate([state_mlstm_conv[l], uqk], axis=1)[:, DS:, :])):
            lst.append(a)

    outs_p = [jnp.stack(a) for a in rec_p]
    outs_s = [jnp.stack(a) for a in rec_s]
    return (xp.reshape(B, T, D), xs.reshape(DB, DS, D), *outs_p, *outs_s)
```

```python
import functools
import math

import jax
import jax.numpy as jnp
from jax import lax
from jax.experimental import pallas as pl
from jax.experimental.pallas import tpu as pltpu

F32 = jnp.float32
BF16 = jnp.bfloat16
I32 = jnp.int32

M_HEADS, M_DH = 4, 64
M_WIDTH = M_HEADS * M_DH
CONV_W = 4
S_HEADS, S_KV_HEADS, S_DH = 8, 2, 64
S_REP = S_HEADS // S_KV_HEADS
S_WIDTH = S_HEADS * S_DH
IDX_HEADS, IDX_DH = 8, 64
TOPK_MAX = 256
DF_HEADS, DF_DH = 4, 32
DF_VDH = 2 * DF_DH
DF_WIDTH = DF_HEADS * DF_VDH
ROPE_THETA = 10000.0
EPS = 1e-6

LANES = 128
VMEM_LIMIT = 56 * 1024 * 1024
INT_MIN = -(2 ** 31)
NEG_BIG = -1e30

C_UQK, C_MV, C_MO, C_SQ, C_SKV, C_IQ, C_DQ, C_DK, C_DV, C_MISC, C_END = (
    0, 512, 768, 1024, 1536, 1792, 2304, 2560, 2816, 3072, 3200)
G_IK, G_IG, G_LF, G_IW = 0, 64, 68, 72

_NT = (((1,), (1,)), ((), ()))
_TN = (((0,), (0,)), ((), ()))


def _cparams(sem):
    return pltpu.CompilerParams(dimension_semantics=sem, vmem_limit_bytes=VMEM_LIMIT)


def _swap_halves(zb, half, first):
    return jnp.where(first, pltpu.roll(zb, LANES - half, 1), pltpu.roll(zb, half, 1))


def _inproj_kernel(x_ref, g_ref, w_ref, gb_ref, c64_ref, s64_ref, c32_ref, s32_ref,
                   uqk_ref, mv_ref, mo_ref, sq_ref, skv_ref, iq_ref, dq_ref, dk_ref, dv_ref,
                   ik_ref, gates_ref):
    x = x_ref[...]
    ms = jnp.mean(x * x, axis=-1, keepdims=True)
    xn = (x * lax.rsqrt(ms + EPS) * g_ref[...]).astype(BF16)
    tm = x.shape[0]
    lane = lax.broadcasted_iota(I32, (tm, LANES), 1)
    first64 = (lane % 64) < 32
    first32 = (lane % 32) < 16
    c64, s64 = c64_ref[...], s64_ref[...]
    c32, s32 = c32_ref[...], s32_ref[...]

    def proj(a, b):
        return jnp.dot(xn, w_ref[:, a:b], preferred_element_type=F32)

    def rope_store(out_ref, z, nblk, c, s, half, first):
        for b in range(nblk):
            zb = z[:, b * LANES:(b + 1) * LANES]
            out_ref[:, b * LANES:(b + 1) * LANES] = zb * c + _swap_halves(zb, half, first) * s

    uqk_ref[...] = proj(C_UQK, C_MV)
    mv_ref[...] = proj(C_MV, C_MO)
    mo_ref[...] = jax.nn.sigmoid(proj(C_MO, C_SQ))
    rope_store(sq_ref, proj(C_SQ, C_SKV), 4, c64, s64, 32, first64)
    zkv = proj(C_SKV, C_IQ)
    rope_store(skv_ref, zkv, 1, c64, s64, 32, first64)
    skv_ref[:, LANES:] = zkv[:, LANES:]
    rope_store(iq_ref, proj(C_IQ, C_DQ), 4, c64, s64, 32, first64)
    rope_store(dq_ref, proj(C_DQ, C_DK), 2, c32, s32, 16, first32)
    rope_store(dk_ref, proj(C_DK, C_DV), 2, c32, s32, 16, first32)
    dv_ref[...] = proj(C_DV, C_MISC)
    zm = proj(C_MISC, C_END) + gb_ref[...]
    roped = zm * c64 + _swap_halves(zm, 32, first64) * s64
    logsig = jnp.minimum(zm, 0.0) - jnp.log1p(jnp.exp(-jnp.abs(zm)))
    is_lf = (lane >= G_LF) & (lane < G_IW)
    gates = jnp.where(lane < G_IG, roped, jnp.where(is_lf, logsig, zm))
    gates_ref[...] = gates
    ik_ref[...] = gates[:, :IDX_DH]


def _inproj(x2d, g, w_perm, gbias, tabs, tm):
    n, d = x2d.shape
    nt = tabs[0].shape[0] // tm
    row = lambda w: pl.BlockSpec((tm, w), lambda i: (i, 0))
    tab = pl.BlockSpec((tm, LANES), lambda i: (i % nt, 0))
    widths = (512, 256, 256, 512, 256, 512, 256, 256, 256, IDX_DH, LANES)
    return pl.pallas_call(
        _inproj_kernel,
        grid=(n // tm,),
        in_specs=[row(d), pl.BlockSpec((1, d), lambda i: (0, 0)),
                  pl.BlockSpec((d, C_END), lambda i: (0, 0)),
                  pl.BlockSpec((1, LANES), lambda i: (0, 0)), tab, tab, tab, tab],
        out_specs=[row(w) for w in widths],
        out_shape=[jax.ShapeDtypeStruct((n, w), F32) for w in widths],
        compiler_params=_cparams(("parallel",)),
        name="inproj",
    )(x2d, g, w_perm, gbias, *tabs)


def _permute_w_in(w):
    d = w.shape[0]
    cols = [w[:, 0:1024], w[:, 1032:1544], w[:, 1544:1800], w[:, 1800:2312], w[:, 2384:3152],
            w[:, 2312:2376], w[:, 1024:1032], w[:, 2376:2384],
            jnp.zeros((d, C_END - 3152), w.dtype)]
    return jnp.concatenate(cols, axis=1).astype(BF16)


def _rope_tables(pos, reps):
    out = []
    for dh in (64, 32):
        inv = ROPE_THETA ** (-jnp.arange(0, dh, 2, dtype=F32) / dh)
        ang = pos.astype(F32)[:, None] * inv[None, :]
        cos, sin = jnp.cos(ang), jnp.sin(ang)
        c = jnp.tile(jnp.concatenate([cos, cos], axis=1), (reps, LANES // dh))
        s = jnp.tile(jnp.concatenate([-sin, sin], axis=1), (reps, LANES // dh))
        out += [c, s]
    return tuple(out)


def _mlstm_kernel(uqk_ref, mv_ref, mo_ref, gc_ref, gr_ref, cw_ref, cb_ref, nrm_ref,
                  cbuf_ref, c0_ref, n0_ref, m0_ref,
                  hm_ref, c_out_ref, n_out_ref, m_out_ref,
                  ext_scr, c_scr, n_scr, m_scr, *, L):
    @pl.when(pl.program_id(1) == 0)
    def _():
        ext_scr[5:8, :] = cbuf_ref[0]
        c_scr[...] = c0_ref[0]
        n_scr[...] = n0_ref[0]
        m_scr[...] = m0_ref[0]

    u = uqk_ref[0]
    ext_scr[8:8 + L, :] = u
    y = (cb_ref[...] + ext_scr[5:5 + L, :] * cw_ref[0:1, :] + ext_scr[6:6 + L, :] * cw_ref[1:2, :]
         + ext_scr[7:7 + L, :] * cw_ref[2:3, :] + u * cw_ref[3:4, :])
    ext_scr[5:8, :] = ext_scr[L + 5:L + 8, :]
    qk = y * jax.nn.sigmoid(y)
    q_all = qk[:, :M_WIDTH]
    k_all = qk[:, M_WIDTH:] * (M_DH ** -0.5)
    v_all = mv_ref[0]
    o_all = mo_ref[0]
    gc = gc_ref[0]
    gr = gr_ref[0]

    row = lax.broadcasted_iota(I32, (L, L), 0)
    col = lax.broadcasted_iota(I32, (L, L), 1)
    tril = row >= col
    tri = tril.astype(F32)
    bc_all = jnp.dot(tri, gc, precision=lax.Precision.HIGHEST, preferred_element_type=F32)
    br_all = lax.dot_general(gr, tri, _NT, precision=lax.Precision.HIGHEST, preferred_element_type=F32)

    for h in range(M_HEADS):
        sl = slice(h * M_DH, (h + 1) * M_DH)
        qh, kh, vh = q_all[:, sl], k_all[:, sl], v_all[:, sl]
        b_col = bc_all[:, G_LF + h:G_LF + h + 1]
        ig_col = gc[:, G_IG + h:G_IG + h + 1]
        b_row = br_all[M_HEADS + h:M_HEADS + h + 1, :]
        ig_row = gr[h:h + 1, :]
        m_prev = m_scr[h]
        c_prev = c_scr[h]
        n_prev = n_scr[h]

        a_col = b_col + m_prev
        dmat = jnp.where(tril, b_col - b_row + ig_row, -jnp.inf)
        m_t = jnp.maximum(a_col, jnp.max(dmat, axis=1, keepdims=True))
        w_inter = jnp.exp(a_col - m_t)
        w_intra = jnp.exp(dmat - m_t)
        qb, kb = qh.astype(BF16), kh.astype(BF16)
        s = lax.dot_general(qb, kb, _NT, preferred_element_type=F32)
        qkw = s * w_intra
        inter = lax.dot_general(qb, c_prev.astype(BF16), _NT, preferred_element_type=F32)
        num = jnp.dot(qkw.astype(BF16), vh.astype(BF16), preferred_element_type=F32) + w_inter * inter
        den = (jnp.sum(qkw, axis=1, keepdims=True)
               + w_inter * jnp.sum(qh * n_prev, axis=1, keepdims=True))
        hh = num / jnp.maximum(jnp.abs(den), jnp.exp(-m_t))

        m_last = m_t[L - 1:L, :]
        wl_inter = w_inter[L - 1:L, :]
        wl_col = jnp.exp(b_col[L - 1:L, :] - b_col + ig_col - m_last)
        vw = (vh * wl_col).astype(BF16)
        c_new = wl_inter * c_prev + lax.dot_general(vw, kb, _TN, preferred_element_type=F32)
        n_new = wl_inter * n_prev + jnp.sum(kh * wl_col, axis=0, keepdims=True)
        c_scr[h] = c_new
        n_scr[h] = n_new
        m_scr[h] = m_last
        c_out_ref[0, h] = c_new
        n_out_ref[0, h] = n_new
        m_out_ref[0, h] = m_last

        hn = hh * lax.rsqrt(jnp.mean(hh * hh, axis=1, keepdims=True) + EPS) * nrm_ref[:, sl]
        hm_ref[0, :, sl] = hn * o_all[:, sl]


def _mlstm(uqk, mv, mo, gates, gates_t, conv_w, conv_b, nrm, cbuf, c0, n0, m0, L):
    b, t, _ = uqk.shape
    tok = lambda w: pl.BlockSpec((1, L, w), lambda i, c: (i, c, 0))
    const2 = lambda a: pl.BlockSpec(a.shape, lambda i, c: (0, 0))
    per_b = lambda a: pl.BlockSpec((1,) + a.shape[1:], lambda i, c: (i,) + (0,) * (a.ndim - 1))
    out_shape = [jax.ShapeDtypeStruct((b, t, M_WIDTH), F32),
                 jax.ShapeDtypeStruct((b, M_HEADS, M_DH, M_DH), F32),
                 jax.ShapeDtypeStruct((b, M_HEADS, 1, M_DH), F32),
                 jax.ShapeDtypeStruct((b, M_HEADS, 1, 1), F32)]
    return pl.pallas_call(
        functools.partial(_mlstm_kernel, L=L),
        grid=(b, t // L),
        in_specs=[tok(2 * M_WIDTH), tok(M_WIDTH), tok(M_WIDTH), tok(LANES),
                  pl.BlockSpec((1, 16, L), lambda i, c: (i, 0, c)),
                  const2(conv_w), const2(conv_b), const2(nrm),
                  per_b(cbuf), per_b(c0), per_b(n0), per_b(m0)],
        out_specs=[tok(M_WIDTH)] + [pl.BlockSpec((1,) + s.shape[1:], lambda i, c: (i, 0, 0, 0))
                                    for s in out_shape[1:]],
        out_shape=out_shape,
        scratch_shapes=[pltpu.VMEM((L + 8, 2 * M_WIDTH), F32),
                        pltpu.VMEM((M_HEADS, M_DH, M_DH), F32),
                        pltpu.VMEM((M_HEADS, 1, M_DH), F32),
                        pltpu.VMEM((M_HEADS, 1, 1), F32)],
        compiler_params=_cparams(("parallel", "arbitrary")),
        name="mlstm",
    )(uqk, mv, mo, gates, gates_t, conv_w, conv_b, nrm, cbuf, c0, n0, m0)


def _sort_key(score):
    bits = pltpu.bitcast(score + 0.0, I32)
    return jnp.where(bits >= 0, bits, bits ^ jnp.int32(0x7FFFFFFF))


def _kth_largest(count_ge, shape, k):
    def body(it, ans):
        bit = jnp.left_shift(jnp.int32(1), 31 - it)
        cand = ans | bit
        cnt = count_ge(cand ^ jnp.int32(INT_MIN))
        return jnp.where(cnt >= k, cand, ans)
    ans = lax.fori_loop(0, 32, body, jnp.zeros(shape, I32))
    return ans ^ jnp.int32(INT_MIN)


def _index_cut(count_eq_below, shape, need, nbits):
    def body(it, p):
        t = p | jnp.left_shift(jnp.int32(1), nbits - 1 - it)
        return jnp.where(count_eq_below(t) < need, t, p)
    return lax.fori_loop(0, nbits, body, jnp.zeros(shape, I32))


def _dsa_prompt_kernel(sq_ref, iq_ref, iwt_ref, ik_ref, kv_ref, o_ref,
                       key_scr, m_scr, l_scr, acc_scr, *, tq, topk, nbits):
    i = pl.program_id(1)
    nblk = i + 1
    tk = tq
    krow = lax.broadcasted_iota(I32, (tk, tq), 0)
    qcol = lax.broadcasted_iota(I32, (tk, tq), 1)
    causal = krow <= qcol

    iq = iq_ref[0].astype(BF16)
    iq_h = [iq[:, h * IDX_DH:(h + 1) * IDX_DH] for h in range(IDX_HEADS)]
    iw = iwt_ref[0]

    def score_block(j, masked):
        off = pl.multiple_of(j * tk, tk)
        ikb = ik_ref[0, pl.ds(off, tk), :].astype(BF16)
        sc = jnp.zeros((tk, tq), F32)
        for h in range(IDX_HEADS):
            d = lax.dot_general(ikb, iq_h[h], _NT, preferred_element_type=F32)
            sc = sc + jnp.maximum(d, 0.0) * iw[h:h + 1, :]
        key = _sort_key(sc)
        if masked:
            key = jnp.where(causal, key, jnp.int32(INT_MIN))
        key_scr[pl.ds(off, tk), :] = key

    lax.fori_loop(0, nblk - 1, lambda j, c: (score_block(j, False), c)[1], 0)
    score_block(nblk - 1, True)

    def count(pred):
        def body(j, acc):
            kb = key_scr[pl.ds(pl.multiple_of(j * tk, tk), tk), :]
            ones = jnp.where(pred(kb, j), 1, 0).astype(I32)
            return acc + jnp.sum(ones.reshape(tk // 8, 8, tq), axis=0)
        acc = lax.fori_loop(0, nblk, body, jnp.zeros((8, tq), I32))
        return jnp.sum(acc, axis=0, keepdims=True)

    thr = _kth_largest(lambda v: count(lambda kb, j: kb >= v), (1, tq), topk)
    need = topk - count(lambda kb, j: kb > thr)
    cut = _index_cut(lambda t: count(lambda kb, j: (kb == thr) & (j * tk + krow < t)), (1, tq), need, nbits)

    sq = (sq_ref[0] * (S_DH ** -0.5)).astype(BF16)
    q_st = [jnp.concatenate([sq[:, (n * S_REP + r) * S_DH:(n * S_REP + r + 1) * S_DH]
                             for r in range(S_REP)], axis=0) for n in range(S_KV_HEADS)]
    m_scr[...] = jnp.full(m_scr.shape, -jnp.inf, F32)
    l_scr[...] = jnp.zeros(l_scr.shape, F32)
    acc_scr[...] = jnp.zeros(acc_scr.shape, F32)

    def attend_block(j, masked):
        off = pl.multiple_of(j * tk, tk)
        kb = key_scr[pl.ds(off, tk), :]
        sel = (kb > thr) | ((kb == thr) & (j * tk + krow <= cut))
        if masked:
            sel = sel & causal
        sel = jnp.concatenate([sel] * S_REP, axis=1)
        kvb = kv_ref[0, pl.ds(off, tk), :].astype(BF16)
        for n in range(S_KV_HEADS):
            kn = kvb[:, n * S_DH:(n + 1) * S_DH]
            vn = kvb[:, (S_KV_HEADS + n) * S_DH:(S_KV_HEADS + n + 1) * S_DH]
            s = jnp.where(sel, lax.dot_general(kn, q_st[n], _NT, preferred_element_type=F32), -jnp.inf)
            m_old = m_scr[n]
            m_new = jnp.maximum(m_old, jnp.max(s, axis=0, keepdims=True))
            m_safe = jnp.where(m_new == -jnp.inf, 0.0, m_new)
            alpha = jnp.exp(m_old - m_safe)
            p = jnp.exp(s - m_safe)
            l_scr[n] = alpha * l_scr[n] + jnp.sum(p, axis=0, keepdims=True)
            acc_scr[n] = alpha * acc_scr[n] + lax.dot_general(vn, p.astype(BF16), _TN,
                                                              preferred_element_type=F32)
            m_scr[n] = m_new

    lax.fori_loop(0, nblk - 1, lambda j, c: (attend_block(j, False), c)[1], 0)
    attend_block(nblk - 1, True)

    for n in range(S_KV_HEADS):
        o_t = acc_scr[n] / l_scr[n]
        for r in range(S_REP):
            h = n * S_REP + r
            o_ref[0, :, h * S_DH:(h + 1) * S_DH] = o_t[:, r * tq:(r + 1) * tq].T


def _dsa_prompt(sq, iq, gates_t, ik, skv, tq):
    b, t, _ = sq.shape
    topk = min(TOPK_MAX, t // 4)
    nbits = max(1, (t - 1).bit_length())
    blk = lambda w: pl.BlockSpec((1, tq, w), lambda bi, i: (bi, i, 0))
    full = lambda w: pl.BlockSpec((1, t, w), lambda bi, i: (bi, 0, 0))
    return pl.pallas_call(
        functools.partial(_dsa_prompt_kernel, tq=tq, topk=topk, nbits=nbits),
        grid=(b, t // tq),
        in_specs=[blk(S_WIDTH), blk(IDX_HEADS * IDX_DH),
                  pl.BlockSpec((1, IDX_HEADS, tq), lambda bi, i: (bi, 1, i)),
                  full(IDX_DH), full(2 * S_KV_HEADS * S_DH)],
        out_specs=blk(S_WIDTH),
        out_shape=jax.ShapeDtypeStruct((b, t, S_WIDTH), F32),
        scratch_shapes=[pltpu.VMEM((t, tq), I32),
                        pltpu.VMEM((S_KV_HEADS, 1, S_REP * tq), F32),
                        pltpu.VMEM((S_KV_HEADS, 1, S_REP * tq), F32),
                        pltpu.VMEM((S_KV_HEADS, S_DH, S_REP * tq), F32)],
        compiler_params=_cparams(("parallel", "parallel")),
        name="dsa_prompt",
    )(sq, iq, gates_t, ik, skv)


def _diff_lambda(lp_ref, lam_init):
    lp = lp_ref[...]
    a = jnp.sum(lp[0:1, :] * lp[1:2, :], axis=1, keepdims=True)
    b = jnp.sum(lp[2:3, :] * lp[3:4, :], axis=1, keepdims=True)
    return jnp.exp(a) - jnp.exp(b) + lam_init


def _diff_prompt_kernel(lp_ref, gn_ref, q_ref, k_ref, v_ref, o_ref, m_scr, l_scr, acc_scr,
                        *, tq, lam_init):
    i = pl.program_id(1)
    tk = tq
    nmap = 2 * DF_HEADS
    krow = lax.broadcasted_iota(I32, (tk, tq), 0)
    qcol = lax.broadcasted_iota(I32, (tk, tq), 1)
    causal = jnp.concatenate([krow <= qcol] * nmap, axis=1)
    q = q_ref[0] * (DF_DH ** -0.5)
    lane_map = lax.broadcasted_iota(I32, q.shape, 1) // DF_DH
    qbd = jnp.concatenate([jnp.where(lane_map == mi, q, 0.0) for mi in range(nmap)], axis=0).astype(BF16)
    m_scr[...] = jnp.full(m_scr.shape, -jnp.inf, F32)
    l_scr[...] = jnp.zeros(l_scr.shape, F32)
    acc_scr[...] = jnp.zeros(acc_scr.shape, F32)

    def block(j, masked):
        off = pl.multiple_of(j * tk, tk)
        kb = k_ref[0, pl.ds(off, tk), :].astype(BF16)
        vb = v_ref[0, pl.ds(off, tk), :].astype(BF16)
        s = lax.dot_general(kb, qbd, _NT, preferred_element_type=F32)
        if masked:
            s = jnp.where(causal, s, -jnp.inf)
        m_old = m_scr[...]
        m_new = jnp.maximum(m_old, jnp.max(s, axis=0, keepdims=True))
        alpha = jnp.exp(m_old - m_new)
        p = jnp.exp(s - m_new)
        l_scr[...] = alpha * l_scr[...] + jnp.sum(p, axis=0, keepdims=True)
        m_scr[...] = m_new
        pb = p.astype(BF16)
        for h in range(DF_HEADS):
            cols = slice(2 * h * tq, (2 * h + 2) * tq)
            acc_scr[h] = alpha[:, cols] * acc_scr[h] + lax.dot_general(
                vb[:, h * DF_VDH:(h + 1) * DF_VDH], pb[:, cols], _TN, preferred_element_type=F32)

    block(i, True)
    lax.fori_loop(0, i, lambda j, c: (block(j, False), c)[1], 0)

    lam = _diff_lambda(lp_ref, lam_init)
    l = l_scr[...]
    for h in range(DF_HEADS):
        a = acc_scr[h]
        o = (a[:, :tq] / l[:, 2 * h * tq:(2 * h + 1) * tq]
             - lam * (a[:, tq:] / l[:, (2 * h + 1) * tq:(2 * h + 2) * tq])).T
        o = o * lax.rsqrt(jnp.mean(o * o, axis=1, keepdims=True) + EPS) * gn_ref[...]
        o_ref[0, :, h * DF_VDH:(h + 1) * DF_VDH] = o * (1.0 - lam_init)


def _diff_prompt(lp, gn, dq, dk, dv, lam_init, tq):
    b, t, w = dq.shape
    blk = pl.BlockSpec((1, tq, w), lambda bi, i: (bi, i, 0))
    full = pl.BlockSpec((1, t, w), lambda bi, i: (bi, 0, 0))
    nmap = 2 * DF_HEADS
    return pl.pallas_call(
        functools.partial(_diff_prompt_kernel, tq=tq, lam_init=lam_init),
        grid=(b, t // tq),
        in_specs=[pl.BlockSpec(lp.shape, lambda bi, i: (0, 0)), pl.BlockSpec(gn.shape, lambda bi, i: (0, 0)),
                  blk, full, full],
        out_specs=blk,
        out_shape=jax.ShapeDtypeStruct((b, t, w), F32),
        scratch_shapes=[pltpu.VMEM((1, nmap * tq), F32), pltpu.VMEM((1, nmap * tq), F32),
                        pltpu.VMEM((DF_HEADS, DF_VDH, 2 * tq), F32)],
        compiler_params=_cparams(("parallel", "parallel")),
        name="diff_prompt",
    )(lp, gn, dq, dk, dv)


def _out_ffn_kernel(x_ref, hm_ref, so_ref, hd_ref, wo_ref, gf_ref, wu_ref, wd_ref, gl_ref, y_ref,
                    xn_scr, *, final_norm):
    c = pl.program_id(1)

    @pl.when(c == 0)
    def _():
        x1 = (x_ref[...]
              + jnp.dot(hm_ref[...].astype(BF16), wo_ref[0:M_WIDTH, :], preferred_element_type=F32)
              + jnp.dot(so_ref[...].astype(BF16), wo_ref[M_WIDTH:M_WIDTH + S_WIDTH, :],
                        preferred_element_type=F32)
              + jnp.dot(hd_ref[...].astype(BF16), wo_ref[M_WIDTH + S_WIDTH:, :], preferred_element_type=F32))
        xn_scr[...] = (x1 * lax.rsqrt(jnp.mean(x1 * x1, axis=-1, keepdims=True) + EPS)
                       * gf_ref[...]).astype(BF16)
        y_ref[...] = x1

    u = jnp.maximum(jnp.dot(xn_scr[...], wu_ref[...], preferred_element_type=F32), 0.0)
    y_ref[...] += jnp.dot((u * u).astype(BF16), wd_ref[...], preferred_element_type=F32)

    if final_norm:
        @pl.when(c == pl.num_programs(1) - 1)
        def _():
            y = y_ref[...]
            y_ref[...] = y * lax.rsqrt(jnp.mean(y * y, axis=-1, keepdims=True) + EPS) * gl_ref[...]


def _out_ffn(x2d, hm, so, hd, wo, gf, wu, wd, gl, final_norm, tm, tf=1024):
    n, d = x2d.shape
    row = lambda w: pl.BlockSpec((tm, w), lambda i, c: (i, 0))
    const = lambda a: pl.BlockSpec(a.shape, lambda i, c: (0, 0))
    return pl.pallas_call(
        functools.partial(_out_ffn_kernel, final_norm=final_norm),
        grid=(n // tm, wu.shape[1] // tf),
        in_specs=[row(d), row(M_WIDTH), row(S_WIDTH), row(DF_WIDTH), const(wo), const(gf),
                  pl.BlockSpec((d, tf), lambda i, c: (0, c)), pl.BlockSpec((tf, d), lambda i, c: (c, 0)),
                  const(gl)],
        out_specs=row(d),
        out_shape=jax.ShapeDtypeStruct((n, d), F32),
        scratch_shapes=[pltpu.VMEM((tm, d), BF16)],
        compiler_params=_cparams(("parallel", "arbitrary")),
        name="out_ffn",
    )(x2d, hm, so, hd, wo, gf, wu, wd, gl)


def _page_specs(n, layer, width, page_rows, pages_per_step):
    return [pl.BlockSpec((1, 1, width, page_rows),
                         functools.partial(lambda b, c, pt, p: (layer, pt[b, c * pages_per_step + p], 0, 0), p=p))
            for p in range(n)]


def _seq_spec(shape):
    return pl.BlockSpec((1,) + shape, lambda b, c, pt: (b,) + (0,) * len(shape))


def _dsa_sample_score_kernel(pt_ref, iq_ref, iw_ref, ikn_ref, *rest, P, page, ds, topk, nbits):
    pages = rest[:P]
    keys_ref, tkeys_ref, thr_ref, cut_ref, key_scr = rest[P:]
    c = pl.program_id(1)
    nc = pl.num_programs(1)
    past = nc * P * page
    iq = iq_ref[0].astype(BF16)
    iw = iw_ref[0]

    def scores(d):
        r = jnp.maximum(d, 0.0) * iw
        return jnp.sum(r.reshape(ds, IDX_HEADS, d.shape[1]), axis=1)

    pad_rows = jnp.full((8 - ds, page), INT_MIN, I32)
    for p in range(P):
        d = jnp.dot(iq, pages[p][0, 0].astype(BF16), preferred_element_type=F32)
        key = jnp.concatenate([_sort_key(scores(d)), pad_rows], axis=0)
        keys_ref[0, :, p * page:(p + 1) * page] = key
        key_scr[:, pl.ds(pl.multiple_of((c * P + p) * page, page), page)] = key

    @pl.when(c == nc - 1)
    def _():
        trow = lax.broadcasted_iota(I32, (ds, page), 0)
        tcol = lax.broadcasted_iota(I32, (ds, page), 1)
        d = lax.dot_general(iq, ikn_ref[0].astype(BF16), _NT, preferred_element_type=F32)
        tkey = jnp.where(tcol <= trow, _sort_key(scores(d)), jnp.int32(INT_MIN))
        tkey = jnp.concatenate([tkey, pad_rows], axis=0)
        tkeys_ref[0] = tkey
        key_scr[:, past:past + page] = tkey
        keys = key_scr[...]
        idx = lax.broadcasted_iota(I32, keys.shape, 1)

        def cnt(pred):
            return jnp.sum(jnp.where(pred, 1, 0).astype(I32), axis=1, keepdims=True)

        thr = _kth_largest(lambda v: cnt(keys >= v), (8, 1), topk)
        need = topk - cnt(keys > thr)
        cut = _index_cut(lambda t: cnt((keys == thr) & (idx < t)), (8, 1), need, nbits)
        thr_ref[0] = jnp.broadcast_to(thr, (8, LANES))
        cut_ref[0] = jnp.broadcast_to(cut, (8, LANES))


def _dsa_sample_score(pt, iq32, iw32, ik_new_pad, pool_ik, layer, ds, P):
    db, n_pages = pt.shape
    page = pool_ik.shape[3]
    past = n_pages * page
    nc = n_pages // P
    topk = min(TOPK_MAX, (past + ds) // 4)
    nbits = max(1, (past + page - 1).bit_length())
    out_shape = [jax.ShapeDtypeStruct((db, 8, past), I32)] + [jax.ShapeDtypeStruct((db, 8, LANES), I32)] * 3
    grid_spec = pltpu.PrefetchScalarGridSpec(
        num_scalar_prefetch=1, grid=(db, nc),
        in_specs=[_seq_spec(iq32.shape[1:]), _seq_spec(iw32.shape[1:]), _seq_spec(ik_new_pad.shape[1:])]
        + _page_specs(P, layer, IDX_DH, page, P),
        out_specs=[pl.BlockSpec((1, 8, P * page), lambda b, c, pt: (b, 0, c))] + [_seq_spec((8, LANES))] * 3,
        scratch_shapes=[pltpu.VMEM((8, past + page), I32)])
    return pl.pallas_call(
        functools.partial(_dsa_sample_score_kernel, P=P, page=page, ds=ds, topk=topk, nbits=nbits),
        grid_spec=grid_spec, out_shape=out_shape,
        compiler_params=_cparams(("parallel", "arbitrary")),
        name="dsa_sample_score",
    )(pt, iq32, iw32, ik_new_pad, *([pool_ik] * P))


def _softmax_step(s, m_scr, l_scr):
    m_old = m_scr[...]
    m_new = jnp.maximum(m_old, jnp.max(s, axis=1, keepdims=True))
    m_safe = jnp.where(m_new == -jnp.inf, 0.0, m_new)
    alpha = jnp.exp(m_old - m_safe)
    p = jnp.exp(s - m_safe)
    l_scr[...] = alpha * l_scr[...] + jnp.sum(p, axis=1, keepdims=True)
    m_scr[...] = m_new
    return alpha, p


def _dsa_sample_attn_kernel(pt_ref, q_ref, keys_ref, tkeys_ref, thr_ref, cut_ref, kvn_ref, *rest,
                            P, page, ds):
    pages = rest[:P]
    o_ref, m_scr, l_scr, acc_scr = rest[P:]
    c = pl.program_id(1)
    nc = pl.num_programs(1)
    past = nc * P * page
    rows = ds * S_HEADS
    kw = S_KV_HEADS * S_DH

    @pl.when(c == 0)
    def _():
        m_scr[...] = jnp.full(m_scr.shape, -jnp.inf, F32)
        l_scr[...] = jnp.zeros(l_scr.shape, F32)
        acc_scr[...] = jnp.zeros(acc_scr.shape, F32)

    q = q_ref[0] * (S_DH ** -0.5)
    rr = lax.broadcasted_iota(I32, (rows, kw), 0)
    ll = lax.broadcasted_iota(I32, (rows, kw), 1)
    own = (ll // S_DH) == ((rr % S_HEADS) // S_REP)
    qbd = jnp.where(own, jnp.concatenate([q] * S_KV_HEADS, axis=1), 0.0).astype(BF16)
    thr = thr_ref[0][:ds, 0:1]
    cut = cut_ref[0][:ds, 0:1]

    def expand(sel):
        return jnp.broadcast_to(sel[:, None, :], (ds, S_HEADS, sel.shape[1])).reshape(rows, sel.shape[1])

    def attend(kv_blocks, sel, paged):
        def qk(kb):
            if paged:
                return jnp.dot(qbd, kb[:kw, :].astype(BF16), preferred_element_type=F32)
            return lax.dot_general(qbd, kb[:, :kw].astype(BF16), _NT, preferred_element_type=F32)

        def pv_of(pb, kb):
            if paged:
                return lax.dot_general(pb, kb[kw:, :].astype(BF16), _NT, preferred_element_type=F32)
            return jnp.dot(pb, kb[:, kw:].astype(BF16), preferred_element_type=F32)

        s = jnp.concatenate([qk(kb) for kb in kv_blocks], axis=1)
        s = jnp.where(expand(sel), s, -jnp.inf)
        alpha, p = _softmax_step(s, m_scr, l_scr)
        pv = jnp.zeros((rows, kw), F32)
        for bi, kb in enumerate(kv_blocks):
            pv = pv + pv_of(p[:, bi * page:(bi + 1) * page].astype(BF16), kb)
        acc_scr[...] = alpha * acc_scr[...] + pv

    keys = keys_ref[0][:ds, :]
    col = lax.broadcasted_iota(I32, keys.shape, 1) + c * (P * page)
    attend([pg[0, 0] for pg in pages], (keys > thr) | ((keys == thr) & (col <= cut)), True)

    @pl.when(c == nc - 1)
    def _():
        tkeys = tkeys_ref[0][:ds, :]
        trow = lax.broadcasted_iota(I32, tkeys.shape, 0)
        tcol = lax.broadcasted_iota(I32, tkeys.shape, 1)
        sel = ((tkeys > thr) | ((tkeys == thr) & (tcol + past <= cut))) & (tcol <= trow)
        attend([kvn_ref[0]], sel, False)
        o = acc_scr[...] / l_scr[...]
        first = ((lax.broadcasted_iota(I32, (rows, S_DH), 0) % S_HEADS) // S_REP) == 0
        o_ref[0] = jnp.where(first, o[:, :S_DH], o[:, S_DH:])


def _dsa_sample_attn(pt, q32, keys, tkeys, thr, cut, kv_new_pad, pool_kv, layer, ds, P):
    db, n_pages = pt.shape
    page = pool_kv.shape[3]
    nc = n_pages // P
    rows = ds * S_HEADS
    kw = S_KV_HEADS * S_DH
    grid_spec = pltpu.PrefetchScalarGridSpec(
        num_scalar_prefetch=1, grid=(db, nc),
        in_specs=[_seq_spec(q32.shape[1:]), pl.BlockSpec((1, 8, P * page), lambda b, c, pt: (b, 0, c)),
                  _seq_spec((8, LANES)), _seq_spec((8, LANES)), _seq_spec((8, LANES)),
                  _seq_spec(kv_new_pad.shape[1:])] + _page_specs(P, layer, 2 * kw, page, P),
        out_specs=_seq_spec((rows, S_DH)),
        scratch_shapes=[pltpu.VMEM((rows, 1), F32), pltpu.VMEM((rows, 1), F32), pltpu.VMEM((rows, kw), F32)])
    return pl.pallas_call(
        functools.partial(_dsa_sample_attn_kernel, P=P, page=page, ds=ds),
        grid_spec=grid_spec, out_shape=jax.ShapeDtypeStruct((db, rows, S_DH), F32),
        compiler_params=_cparams(("parallel", "arbitrary")),
        name="dsa_sample_attn",
    )(pt, q32, keys, tkeys, thr, cut, kv_new_pad, *([pool_kv] * P))


def _diff_sample_kernel(pt_ref, lp_ref, gn_ref, q_ref, kn_ref, vn_ref, *rest, P, page, ds, lam_init):
    kpages, vpages = rest[:P], rest[P:2 * P]
    o_ref, m_scr, l_scr, acc_scr = rest[2 * P:]
    c = pl.program_id(1)
    nc = pl.num_programs(1)
    nmap = 2 * DF_HEADS
    rows = ds * nmap
    half = rows // 2

    @pl.when(c == 0)
    def _():
        m_scr[...] = jnp.full(m_scr.shape, -jnp.inf, F32)
        l_scr[...] = jnp.zeros(l_scr.shape, F32)
        acc_scr[...] = jnp.zeros(acc_scr.shape, F32)

    q = q_ref[0] * (DF_DH ** -0.5)
    rr = lax.broadcasted_iota(I32, (rows, DF_WIDTH), 0)
    ll = lax.broadcasted_iota(I32, (rows, DF_WIDTH), 1)
    own = (ll // DF_DH) == (2 * (rr % DF_HEADS) + rr // half)
    qbd = jnp.where(own, jnp.concatenate([q] * nmap, axis=1), 0.0).astype(BF16)

    def attend(kblocks, vblocks, mask, paged):
        if paged:
            s = jnp.concatenate([jnp.dot(qbd, kb.astype(BF16), preferred_element_type=F32)
                                 for kb in kblocks], axis=1)
        else:
            s = jnp.concatenate([lax.dot_general(qbd, kb.astype(BF16), _NT, preferred_element_type=F32)
                                 for kb in kblocks], axis=1)
        if mask is not None:
            s = jnp.where(mask, s, -jnp.inf)
        alpha, p = _softmax_step(s, m_scr, l_scr)
        pv = jnp.zeros((rows, DF_WIDTH), F32)
        for bi, vb in enumerate(vblocks):
            pb = p[:, bi * page:(bi + 1) * page].astype(BF16)
            if paged:
                pv = pv + lax.dot_general(pb, vb.astype(BF16), _NT, preferred_element_type=F32)
            else:
                pv = pv + jnp.dot(pb, vb.astype(BF16), preferred_element_type=F32)
        acc_scr[...] = alpha * acc_scr[...] + pv

    attend([pg[0, 0] for pg in kpages], [pg[0, 0] for pg in vpages], None, True)

    @pl.when(c == nc - 1)
    def _():
        trow = (lax.broadcasted_iota(I32, (rows, page), 0) % half) // DF_HEADS
        tcol = lax.broadcasted_iota(I32, (rows, page), 1)
        attend([kn_ref[0]], [vn_ref[0]], tcol <= trow, False)
        o = acc_scr[...] / l_scr[...]
        hrow = lax.broadcasted_iota(I32, (rows, DF_VDH), 0) % DF_HEADS
        oh = jnp.zeros((rows, DF_VDH), F32)
        for h in range(DF_HEADS):
            oh = jnp.where(hrow == h, o[:, h * DF_VDH:(h + 1) * DF_VDH], oh)
        lam = _diff_lambda(lp_ref, lam_init)
        d = oh[:half, :] - lam * oh[half:, :]
        d = d * lax.rsqrt(jnp.mean(d * d, axis=1, keepdims=True) + EPS) * gn_ref[...]
        o_ref[0] = d * (1.0 - lam_init)


def _diff_sample(pt, lp, gn, q32, k_new_pad, v_new_pad, pool_k, pool_v, layer, lam_init, ds, P):
    db, n_pages = pt.shape
    page = pool_k.shape[3]
    nc = n_pages // P
    rows = ds * 2 * DF_HEADS
    const = lambda a: pl.BlockSpec(a.shape, lambda b, c, pt: (0, 0))
    grid_spec = pltpu.PrefetchScalarGridSpec(
        num_scalar_prefetch=1, grid=(db, nc),
        in_specs=[const(lp), const(gn), _seq_spec(q32.shape[1:]), _seq_spec(k_new_pad.shape[1:]),
                  _seq_spec(v_new_pad.shape[1:])]
        + _page_specs(P, layer, DF_WIDTH, page, P) + _page_specs(P, layer, DF_WIDTH, page, P),
        out_specs=_seq_spec((rows // 2, DF_VDH)),
        scratch_shapes=[pltpu.VMEM((rows, 1), F32), pltpu.VMEM((rows, 1), F32),
                        pltpu.VMEM((rows, DF_WIDTH), F32)])
    return pl.pallas_call(
        functools.partial(_diff_sample_kernel, P=P, page=page, ds=ds, lam_init=lam_init),
        grid_spec=grid_spec, out_shape=jax.ShapeDtypeStruct((db, rows // 2, DF_VDH), F32),
        compiler_params=_cparams(("parallel", "arbitrary")),
        name="diff_sample",
    )(pt, lp, gn, q32, k_new_pad, v_new_pad, *([pool_k] * P), *([pool_v] * P))


def _largest_tile(n, cap):
    t = cap
    while n % t:
        t //= 2
    return t


def _pad_time(a, to, value=0.0):
    return jnp.pad(a, ((0, 0), (0, to - a.shape[1]), (0, 0)), constant_values=value)


def kernel(x_prompt, x_sample, cache_dsa_kv, cache_idx_k, cache_diff_k, cache_diff_v, state_mlstm_C, state_mlstm_n, state_mlstm_m, state_mlstm_conv, page_table, norm_mix, w_in, mlstm_conv_w, mlstm_conv_b, mlstm_gate_b, mlstm_norm, diff_lambda, diff_norm, w_out, norm_ffn, w_up, w_down, norm_final):
    B, T, D = x_prompt.shape
    DB, DS, _ = x_sample.shape
    depth = w_in.shape[0]
    n_pool, page = cache_idx_k.shape[1:3]
    past = page_table.shape[1] * page
    P = _largest_tile(page_table.shape[1], 16)
    LS = LANES

    tm_p = _largest_tile(T, 512)
    tabs_p = _rope_tables(jnp.arange(T), 1)
    tabs_s = _rope_tables(past + jnp.arange(DS), DB)
    gate_pad = jnp.zeros((LANES,), F32).at[G_IG:G_IG + M_HEADS].set(NEG_BIG)
    pool_ik = jnp.swapaxes(cache_idx_k, 2, 3)
    pool_kv = jnp.transpose(cache_dsa_kv, (0, 1, 3, 4, 5, 2)).reshape(depth, n_pool, -1, page)
    pool_dk = jnp.transpose(cache_diff_k, (0, 1, 3, 4, 5, 2)).reshape(depth, n_pool, -1, page)
    pool_dv = jnp.transpose(cache_diff_v, (0, 1, 3, 4, 2)).reshape(depth, n_pool, -1, page)

    xp = x_prompt.reshape(B * T, D)
    xs = x_sample.reshape(DB * DS, D)
    rec_p = [[] for _ in range(8)]
    rec_s = [[] for _ in range(8)]
    for l in range(depth):
        lam_init = 0.8 - 0.6 * math.exp(-0.3 * l)
        last = l == depth - 1
        w_perm = _permute_w_in(w_in[l])
        g_mix = norm_mix[l][None, :]
        gbias = jnp.zeros((1, LANES), F32).at[0, G_IG:G_IG + M_HEADS].set(mlstm_gate_b[l, 0]) \
            .at[0, G_LF:G_LF + M_HEADS].set(mlstm_gate_b[l, 1])
        conv_w, conv_b = mlstm_conv_w[l], mlstm_conv_b[l][None, :]
        nrm_m = mlstm_norm[l][None, :]
        lp, gn = diff_lambda[l], diff_norm[l][None, :]
        wo, wu, wd = w_out[l].astype(BF16), w_up[l].astype(BF16), w_down[l].astype(BF16)
        g_ffn, g_fin = norm_ffn[l][None, :], norm_final[None, :]

        (uqk, mv, mo, sq, skv, iq, dq, dk, dv, ik, gates) = [
            a.reshape(B, T, -1) for a in _inproj(xp, g_mix, w_perm, gbias, tabs_p, tm_p)]
        gates_t = jnp.swapaxes(gates[:, :, G_IG:G_IG + 16], 1, 2)
        hm, C, n, m = _mlstm(uqk, mv, mo, gates, gates_t, conv_w, conv_b, nrm_m,
                             jnp.zeros((B, CONV_W - 1, 2 * M_WIDTH), F32),
                             jnp.zeros((B, M_HEADS, M_DH, M_DH), F32), jnp.zeros((B, M_HEADS, 1, M_DH), F32),
                             jnp.zeros((B, M_HEADS, 1, 1), F32), _largest_tile(T, 256))
        s_out = _dsa_prompt(sq, iq, gates_t, ik, skv, _largest_tile(T, 256))
        hd = _diff_prompt(lp, gn, dq, dk, dv, lam_init, _largest_tile(T, 256))
        xp = _out_ffn(xp, hm.reshape(B * T, -1), s_out.reshape(B * T, -1), hd.reshape(B * T, -1),
                      wo, g_ffn, wu, wd, g_fin, last, tm_p)
        for lst, a in zip(rec_p, (skv.reshape(B, T, 2, S_KV_HEADS, S_DH), ik,
                                  dk.reshape(B, T, DF_HEADS, 2, DF_DH), dv.reshape(B, T, DF_HEADS, DF_VDH),
                                  C, n.reshape(B, M_HEADS, M_DH), m.reshape(B, M_HEADS),
                                  uqk[:, T - (CONV_W - 1):, :])):
            lst.append(a)

        (uqk, mv, mo, sq, skv, iq, dq, dk, dv, ik, gates) = [
            a.reshape(DB, DS, -1) for a in _inproj(xs, g_mix, w_perm, gbias, tabs_s, DB * DS)]
        gates_pd = jnp.concatenate([gates, jnp.broadcast_to(gate_pad, (DB, LS - DS, LANES))], axis=1)
        gates_t = jnp.swapaxes(gates_pd[:, :, G_IG:G_IG + 16], 1, 2)
        hm, C, n, m = _mlstm(_pad_time(uqk, LS), _pad_time(mv, LS), _pad_time(mo, LS), gates_pd, gates_t,
                             conv_w, conv_b, nrm_m, state_mlstm_conv[l], state_mlstm_C[l],
                             state_mlstm_n[l][:, :, None, :], state_mlstm_m[l][:, :, None, None], LS)
        hm = hm[:, :DS, :]
        keys, tkeys, thr, cut = _dsa_sample_score(
            page_table, iq.reshape(DB, DS * IDX_HEADS, IDX_DH),
            gates[:, :, G_IW:G_IW + IDX_HEADS].reshape(DB, DS * IDX_HEADS, 1),
            _pad_time(ik, page), pool_ik, l, DS, P)
        s_out = _dsa_sample_attn(page_table, sq.reshape(DB, DS * S_HEADS, S_DH), keys, tkeys, thr, cut,
                                 _pad_time(skv, page), pool_kv, l, DS, P)
        dq32 = dq.reshape(DB, DS, DF_HEADS, 2, DF_DH).transpose(0, 3, 1, 2, 4).reshape(DB, -1, DF_DH)
        hd = _diff_sample(page_table, lp, gn, dq32, _pad_time(dk, page), _pad_time(dv, page),
                          pool_dk, pool_dv, l, lam_init, DS, P)
        xs = _out_ffn(xs, hm.reshape(DB * DS, -1), s_out.reshape(DB * DS, -1), hd.reshape(DB * DS, -1),
                      wo, g_ffn, wu, wd, g_fin, last, DB * DS)
        for lst, a in zip(rec_s, (skv.reshape(DB, DS, 2, S_KV_HEADS, S_DH), ik,
                                  dk.reshape(DB, DS, DF_HEADS, 2, DF_DH), dv.reshape(DB, DS, DF_HEADS, DF_VDH),
                                  C, n.reshape(DB, M_HEADS, M_DH), m.reshape(DB, M_HEADS),
                                  jnp.concatenate([state_mlstm_conv[l], uqk], axis=1)[:, DS:, :])):
            lst.append(a)

    outs_p = [jnp.stack(a) for a in rec_p]
    outs_s = [jnp.stack(a) for a in rec_s]
    return (xp.reshape(B, T, D), xs.reshape(DB, DS, D), *outs_p, *outs_s)
```

```python
import functools
import math

import jax
import jax.numpy as jnp
from jax import lax
from jax.experimental import pallas as pl
from jax.experimental.pallas import tpu as pltpu

F32 = jnp.float32
BF16 = jnp.bfloat16
I32 = jnp.int32

M_HEADS, M_DH = 4, 64
M_WIDTH = M_HEADS * M_DH
CONV_W = 4
S_HEADS, S_KV_HEADS, S_DH = 8, 2, 64
S_REP = S_HEADS // S_KV_HEADS
S_WIDTH = S_HEADS * S_DH
IDX_HEADS, IDX_DH = 8, 64
TOPK_MAX = 256
DF_HEADS, DF_DH = 4, 32
DF_VDH = 2 * DF_DH
DF_WIDTH = DF_HEADS * DF_VDH
ROPE_THETA = 10000.0
EPS = 1e-6

LANES = 128
VMEM_LIMIT = 56 * 1024 * 1024
INT_MIN = -(2 ** 31)
NEG_BIG = -1e30

C_UQK, C_MV, C_MO, C_SQ, C_SKV, C_IQ, C_DQ, C_DK, C_DV, C_MISC, C_END = (
    0, 512, 768, 1024, 1536, 1792, 2304, 2560, 2816, 3072, 3200)
G_IK, G_IG, G_LF, G_IW = 0, 64, 68, 72

_NT = (((1,), (1,)), ((), ()))
_TN = (((0,), (0,)), ((), ()))


def _cparams(sem):
    return pltpu.CompilerParams(dimension_semantics=sem, vmem_limit_bytes=VMEM_LIMIT)


def _swap_halves(zb, half, first):
    return jnp.where(first, pltpu.roll(zb, LANES - half, 1), pltpu.roll(zb, half, 1))


def _inproj_kernel(x_ref, g_ref, w_ref, gb_ref, c64_ref, s64_ref, c32_ref, s32_ref,
                   uqk_ref, mv_ref, mo_ref, sq_ref, skv_ref, iq_ref, dq_ref, dk_ref, dv_ref,
                   ik_ref, gates_ref, ikb_ref, kvb_ref, dkb_ref, dvb_ref):
    x = x_ref[...]
    ms = jnp.mean(x * x, axis=-1, keepdims=True)
    xn = (x * lax.rsqrt(ms + EPS) * g_ref[...]).astype(BF16)
    tm = x.shape[0]
    lane = lax.broadcasted_iota(I32, (tm, LANES), 1)
    first64 = (lane % 64) < 32
    first32 = (lane % 32) < 16
    c64, s64 = c64_ref[...], s64_ref[...]
    c32, s32 = c32_ref[...], s32_ref[...]

    def proj(a, b):
        return jnp.dot(xn, w_ref[:, a:b], preferred_element_type=F32)

    def rope_store(out_ref, z, nblk, c, s, half, first):
        for b in range(nblk):
            zb = z[:, b * LANES:(b + 1) * LANES]
            out_ref[:, b * LANES:(b + 1) * LANES] = zb * c + _swap_halves(zb, half, first) * s

    uqk_ref[...] = proj(C_UQK, C_MV)
    mv_ref[...] = proj(C_MV, C_MO)
    mo_ref[...] = jax.nn.sigmoid(proj(C_MO, C_SQ))
    rope_store(sq_ref, proj(C_SQ, C_SKV), 4, c64, s64, 32, first64)
    zkv = proj(C_SKV, C_IQ)
    rope_store(skv_ref, zkv, 1, c64, s64, 32, first64)
    skv_ref[:, LANES:] = zkv[:, LANES:]
    rope_store(iq_ref, proj(C_IQ, C_DQ), 4, c64, s64, 32, first64)
    rope_store(dq_ref, proj(C_DQ, C_DK), 2, c32, s32, 16, first32)
    rope_store(dk_ref, proj(C_DK, C_DV), 2, c32, s32, 16, first32)
    dv_ref[...] = proj(C_DV, C_MISC)
    zm = proj(C_MISC, C_END) + gb_ref[...]
    roped = zm * c64 + _swap_halves(zm, 32, first64) * s64
    logsig = jnp.minimum(zm, 0.0) - jnp.log1p(jnp.exp(-jnp.abs(zm)))
    is_lf = (lane >= G_LF) & (lane < G_IW)
    gates = jnp.where(lane < G_IG, roped, jnp.where(is_lf, logsig, zm))
    gates_ref[...] = gates
    ik_ref[...] = gates[:, :IDX_DH]

    low = lane < 64
    ikb_ref[...] = gates[:, :IDX_DH].astype(BF16)
    dkb_ref[...] = dk_ref[...].astype(BF16)
    kro, vkv = skv_ref[:, :LANES], zkv[:, LANES:]
    kvb_ref[:, 0 * LANES:1 * LANES] = jnp.where(low, kro, 0.0).astype(BF16)
    kvb_ref[:, 1 * LANES:2 * LANES] = jnp.where(low, pltpu.roll(kro, 64, 1), 0.0).astype(BF16)
    kvb_ref[:, 2 * LANES:3 * LANES] = jnp.where(low, vkv, 1.0).astype(BF16)
    kvb_ref[:, 3 * LANES:4 * LANES] = jnp.where(low, pltpu.roll(vkv, 64, 1), 1.0).astype(BF16)
    for b in range(DF_WIDTH // LANES):
        vb = dv_ref[:, b * LANES:(b + 1) * LANES]
        dvb_ref[:, (2 * b) * LANES:(2 * b + 1) * LANES] = jnp.where(low, vb, 1.0).astype(BF16)
        dvb_ref[:, (2 * b + 1) * LANES:(2 * b + 2) * LANES] = jnp.where(
            low, pltpu.roll(vb, 64, 1), 1.0).astype(BF16)


def _inproj(x2d, g, w_perm, gbias, tabs, tm):
    n, d = x2d.shape
    nt = tabs[0].shape[0] // tm
    row = lambda w: pl.BlockSpec((tm, w), lambda i: (i, 0))
    tab = pl.BlockSpec((tm, LANES), lambda i: (i % nt, 0))
    widths = (512, 256, 256, 512, 256, 512, 256, 256, 256, IDX_DH, LANES)
    widths_b = (IDX_DH, 4 * LANES, DF_WIDTH, 2 * DF_WIDTH)
    return pl.pallas_call(
        _inproj_kernel,
        grid=(n // tm,),
        in_specs=[row(d), pl.BlockSpec((1, d), lambda i: (0, 0)),
                  pl.BlockSpec((d, C_END), lambda i: (0, 0)),
                  pl.BlockSpec((1, LANES), lambda i: (0, 0)), tab, tab, tab, tab],
        out_specs=[row(w) for w in widths + widths_b],
        out_shape=[jax.ShapeDtypeStruct((n, w), F32) for w in widths]
        + [jax.ShapeDtypeStruct((n, w), BF16) for w in widths_b],
        compiler_params=_cparams(("parallel",)),
        name="inproj",
    )(x2d, g, w_perm, gbias, *tabs)


def _permute_w_in(w):
    d = w.shape[0]
    cols = [w[:, 0:1024], w[:, 1032:1544], w[:, 1544:1800], w[:, 1800:2312], w[:, 2384:3152],
            w[:, 2312:2376], w[:, 1024:1032], w[:, 2376:2384],
            jnp.zeros((d, C_END - 3152), w.dtype)]
    return jnp.concatenate(cols, axis=1).astype(BF16)


def _rope_tables(pos, reps):
    out = []
    for dh in (64, 32):
        inv = ROPE_THETA ** (-jnp.arange(0, dh, 2, dtype=F32) / dh)
        ang = pos.astype(F32)[:, None] * inv[None, :]
        cos, sin = jnp.cos(ang), jnp.sin(ang)
        c = jnp.tile(jnp.concatenate([cos, cos], axis=1), (reps, LANES // dh))
        s = jnp.tile(jnp.concatenate([-sin, sin], axis=1), (reps, LANES // dh))
        out += [c, s]
    return tuple(out)


def _mlstm_kernel(uqk_ref, mv_ref, mo_ref, gc_ref, gr_ref, cw_ref, cb_ref, nrm_ref,
                  cbuf_ref, c0_ref, n0_ref, m0_ref,
                  hm_ref, c_out_ref, n_out_ref, m_out_ref,
                  ext_scr, c_scr, n_scr, m_scr, *, L):
    @pl.when(pl.program_id(1) == 0)
    def _():
        ext_scr[5:8, :] = cbuf_ref[0]
        c_scr[...] = c0_ref[0]
        n_scr[...] = n0_ref[0]
        m_scr[...] = m0_ref[0]

    u = uqk_ref[0]
    ext_scr[8:8 + L, :] = u
    y = (cb_ref[...] + ext_scr[5:5 + L, :] * cw_ref[0:1, :] + ext_scr[6:6 + L, :] * cw_ref[1:2, :]
         + ext_scr[7:7 + L, :] * cw_ref[2:3, :] + u * cw_ref[3:4, :])
    ext_scr[5:8, :] = ext_scr[L + 5:L + 8, :]
    qk = y * jax.nn.sigmoid(y)
    q_all = qk[:, :M_WIDTH]
    k_all = qk[:, M_WIDTH:] * (M_DH ** -0.5)
    v_all = mv_ref[0]
    o_all = mo_ref[0]
    gc = gc_ref[0]
    gr = gr_ref[0]

    row = lax.broadcasted_iota(I32, (L, L), 0)
    col = lax.broadcasted_iota(I32, (L, L), 1)
    tril = row >= col
    tri = tril.astype(F32)
    bc_all = jnp.dot(tri, gc, precision=lax.Precision.HIGHEST, preferred_element_type=F32)
    br_all = lax.dot_general(gr, tri, _NT, precision=lax.Precision.HIGHEST, preferred_element_type=F32)

    for h in range(M_HEADS):
        sl = slice(h * M_DH, (h + 1) * M_DH)
        qh, kh, vh = q_all[:, sl], k_all[:, sl], v_all[:, sl]
        b_col = bc_all[:, G_LF + h:G_LF + h + 1]
        ig_col = gc[:, G_IG + h:G_IG + h + 1]
        b_row = br_all[M_HEADS + h:M_HEADS + h + 1, :]
        ig_row = gr[h:h + 1, :]
        m_prev = m_scr[h]
        c_prev = c_scr[h]
        n_prev = n_scr[h]

        a_col = b_col + m_prev
        dmat = jnp.where(tril, b_col - b_row + ig_row, -jnp.inf)
        m_t = jnp.maximum(a_col, jnp.max(dmat, axis=1, keepdims=True))
        w_inter = jnp.exp(a_col - m_t)
        w_intra = jnp.exp(dmat - m_t)
        qb, kb = qh.astype(BF16), kh.astype(BF16)
        s = lax.dot_general(qb, kb, _NT, preferred_element_type=F32)
        qkw = s * w_intra
        inter = lax.dot_general(qb, c_prev.astype(BF16), _NT, preferred_element_type=F32)
        num = jnp.dot(qkw.astype(BF16), vh.astype(BF16), preferred_element_type=F32) + w_inter * inter
        den = (jnp.sum(qkw, axis=1, keepdims=True)
               + w_inter * jnp.sum(qh * n_prev, axis=1, keepdims=True))
        hh = num / jnp.maximum(jnp.abs(den), jnp.exp(-m_t))

        m_last = m_t[L - 1:L, :]
        wl_inter = w_inter[L - 1:L, :]
        wl_col = jnp.exp(b_col[L - 1:L, :] - b_col + ig_col - m_last)
        vw = (vh * wl_col).astype(BF16)
        c_new = wl_inter * c_prev + lax.dot_general(vw, kb, _TN, preferred_element_type=F32)
        n_new = wl_inter * n_prev + jnp.sum(kh * wl_col, axis=0, keepdims=True)
        c_scr[h] = c_new
        n_scr[h] = n_new
        m_scr[h] = m_last
        c_out_ref[0, h] = c_new
        n_out_ref[0, h] = n_new
        m_out_ref[0, h] = m_last

        hn = hh * lax.rsqrt(jnp.mean(hh * hh, axis=1, keepdims=True) + EPS) * nrm_ref[:, sl]
        hm_ref[0, :, sl] = hn * o_all[:, sl]


def _mlstm(uqk, mv, mo, gates, gates_t, conv_w, conv_b, nrm, cbuf, c0, n0, m0, L):
    b, t, _ = uqk.shape
    tok = lambda w: pl.BlockSpec((1, L, w), lambda i, c: (i, c, 0))
    const2 = lambda a: pl.BlockSpec(a.shape, lambda i, c: (0, 0))
    per_b = lambda a: pl.BlockSpec((1,) + a.shape[1:], lambda i, c: (i,) + (0,) * (a.ndim - 1))
    out_shape = [jax.ShapeDtypeStruct((b, t, M_WIDTH), F32),
                 jax.ShapeDtypeStruct((b, M_HEADS, M_DH, M_DH), F32),
                 jax.ShapeDtypeStruct((b, M_HEADS, 1, M_DH), F32),
                 jax.ShapeDtypeStruct((b, M_HEADS, 1, 1), F32)]
    return pl.pallas_call(
        functools.partial(_mlstm_kernel, L=L),
        grid=(b, t // L),
        in_specs=[tok(2 * M_WIDTH), tok(M_WIDTH), tok(M_WIDTH), tok(LANES),
                  pl.BlockSpec((1, 16, L), lambda i, c: (i, 0, c)),
                  const2(conv_w), const2(conv_b), const2(nrm),
                  per_b(cbuf), per_b(c0), per_b(n0), per_b(m0)],
        out_specs=[tok(M_WIDTH)] + [pl.BlockSpec((1,) + s.shape[1:], lambda i, c: (i, 0, 0, 0))
                                    for s in out_shape[1:]],
        out_shape=out_shape,
        scratch_shapes=[pltpu.VMEM((L + 8, 2 * M_WIDTH), F32),
                        pltpu.VMEM((M_HEADS, M_DH, M_DH), F32),
                        pltpu.VMEM((M_HEADS, 1, M_DH), F32),
                        pltpu.VMEM((M_HEADS, 1, 1), F32)],
        compiler_params=_cparams(("parallel", "arbitrary")),
        name="mlstm",
    )(uqk, mv, mo, gates, gates_t, conv_w, conv_b, nrm, cbuf, c0, n0, m0)


def _sort_key(score):
    bits = pltpu.bitcast(score + 0.0, I32)
    return jnp.where(bits >= 0, bits, bits ^ jnp.int32(0x7FFFFFFF))


def _kth_largest(count_ge, shape, k):
    def body(it, ans):
        bit = jnp.left_shift(jnp.int32(1), 31 - it)
        cand = ans | bit
        cnt = count_ge(cand ^ jnp.int32(INT_MIN))
        return jnp.where(cnt >= k, cand, ans)
    ans = lax.fori_loop(0, 32, body, jnp.zeros(shape, I32))
    return ans ^ jnp.int32(INT_MIN)


def _index_cut(count_eq_below, shape, need, nbits):
    def body(it, p):
        t = p | jnp.left_shift(jnp.int32(1), nbits - 1 - it)
        return jnp.where(count_eq_below(t) < need, t, p)
    return lax.fori_loop(0, nbits, body, jnp.zeros(shape, I32))


def _dsa_prompt_kernel(sq_ref, iq_ref, iwt_ref, ik_ref, kv_ref, o_ref,
                       key_scr, m_scr, acc_scr, *, tq, topk, nbits):
    i = pl.program_id(1)
    nblk = i + 1
    tk = tq
    krow = lax.broadcasted_iota(I32, (tk, tq), 0)
    qcol = lax.broadcasted_iota(I32, (tk, tq), 1)
    causal = krow <= qcol

    iq = iq_ref[0].astype(BF16)
    iq_h = [iq[:, h * IDX_DH:(h + 1) * IDX_DH] for h in range(IDX_HEADS)]
    iw = iwt_ref[0]

    def score_block(j, masked):
        off = pl.multiple_of(j * tk, tk)
        ikb = ik_ref[0, pl.ds(off, tk), :]
        sc = jnp.zeros((tk, tq), F32)
        for h in range(IDX_HEADS):
            d = lax.dot_general(ikb, iq_h[h], _NT, preferred_element_type=F32)
            sc = sc + jnp.maximum(d, 0.0) * iw[h:h + 1, :]
        key = _sort_key(sc)
        if masked:
            key = jnp.where(causal, key, jnp.int32(INT_MIN))
        key_scr[pl.ds(off, tk), :] = key

    lax.fori_loop(0, nblk - 1, lambda j, c: (score_block(j, False), c)[1], 0)
    score_block(nblk - 1, True)

    def count(pred):
        def body(j, acc):
            kb = key_scr[pl.ds(pl.multiple_of(j * tk, tk), tk), :]
            ones = jnp.where(pred(kb, j), 1, 0).astype(I32)
            return acc + jnp.sum(ones.reshape(tk // 8, 8, tq), axis=0)
        acc = lax.fori_loop(0, nblk, body, jnp.zeros((8, tq), I32))
        return jnp.sum(acc, axis=0, keepdims=True)

    thr = _kth_largest(lambda v: count(lambda kb, j: kb >= v), (1, tq), topk)
    need = topk - count(lambda kb, j: kb > thr)
    n_eq = count(lambda kb, j: kb == thr)
    tied = jnp.max(jnp.where(n_eq > need, 1, 0).astype(I32))
    cut = lax.cond(
        tied > 0,
        lambda: _index_cut(lambda t: count(lambda kb, j: (kb == thr) & (j * tk + krow < t)),
                           (1, tq), need, nbits),
        lambda: jnp.full((1, tq), (1 << nbits) - 1, I32))

    sq = sq_ref[0] * (S_DH ** -0.5 * math.log2(math.e))
    zpad = jnp.zeros((tq, LANES - S_DH), F32)
    q_st = [jnp.concatenate([jnp.concatenate([sq[:, (n * S_REP + r) * S_DH:(n * S_REP + r + 1) * S_DH], zpad],
                                             axis=1) for r in range(S_REP)], axis=0).astype(BF16)
            for n in range(S_KV_HEADS)]
    m_scr[...] = jnp.full(m_scr.shape, -jnp.inf, F32)
    acc_scr[...] = jnp.zeros(acc_scr.shape, F32)

    def attend_block(j, masked):
        off = pl.multiple_of(j * tk, tk)
        kb = key_scr[pl.ds(off, tk), :]
        sel = (kb > thr) | ((kb == thr) & (j * tk + krow <= cut))
        if masked:
            sel = sel & causal
        sel = jnp.concatenate([sel] * S_REP, axis=1)
        for n in range(S_KV_HEADS):
            kn = kv_ref[0, pl.ds(off, tk), n * LANES:(n + 1) * LANES]
            vn = kv_ref[0, pl.ds(off, tk), (S_KV_HEADS + n) * LANES:(S_KV_HEADS + n + 1) * LANES]
            s = jnp.where(sel, lax.dot_general(kn, q_st[n], _NT, preferred_element_type=F32), -jnp.inf)
            m_old = m_scr[n]
            m_new = jnp.maximum(m_old, jnp.max(s, axis=0, keepdims=True))
            m_safe = jnp.where(m_new == -jnp.inf, 0.0, m_new)
            alpha = jnp.exp2(m_old - m_safe)
            p = jnp.exp2(s - m_safe).astype(BF16)
            acc_scr[n] = alpha * acc_scr[n] + lax.dot_general(vn, p, _TN, preferred_element_type=F32)
            m_scr[n] = m_new

    lax.fori_loop(0, nblk - 1, lambda j, c: (attend_block(j, False), c)[1], 0)
    attend_block(nblk - 1, True)

    for n in range(S_KV_HEADS):
        a = acc_scr[n]
        o_t = a[:S_DH, :] / a[S_DH:S_DH + 1, :]
        for r in range(S_REP):
            h = n * S_REP + r
            o_ref[0, :, h * S_DH:(h + 1) * S_DH] = o_t[:, r * tq:(r + 1) * tq].T


def _dsa_prompt(sq, iq, gates_t, ikb, kvb, tq):
    b, t, _ = sq.shape
    topk = min(TOPK_MAX, t // 4)
    nbits = max(1, (t - 1).bit_length())
    blk = lambda w: pl.BlockSpec((1, tq, w), lambda bi, i: (bi, i, 0))
    full = lambda w: pl.BlockSpec((1, t, w), lambda bi, i: (bi, 0, 0))
    return pl.pallas_call(
        functools.partial(_dsa_prompt_kernel, tq=tq, topk=topk, nbits=nbits),
        grid=(b, t // tq),
        in_specs=[blk(S_WIDTH), blk(IDX_HEADS * IDX_DH),
                  pl.BlockSpec((1, IDX_HEADS, tq), lambda bi, i: (bi, 1, i)),
                  full(IDX_DH), full(kvb.shape[2])],
        out_specs=blk(S_WIDTH),
        out_shape=jax.ShapeDtypeStruct((b, t, S_WIDTH), F32),
        scratch_shapes=[pltpu.VMEM((t, tq), I32),
                        pltpu.VMEM((S_KV_HEADS, 1, S_REP * tq), F32),
                        pltpu.VMEM((S_KV_HEADS, LANES, S_REP * tq), F32)],
        compiler_params=_cparams(("parallel", "parallel")),
        name="dsa_prompt",
    )(sq, iq, gates_t, ikb, kvb)


def _diff_lambda(lp_ref, lam_init):
    lp = lp_ref[...]
    a = jnp.sum(lp[0:1, :] * lp[1:2, :], axis=1, keepdims=True)
    b = jnp.sum(lp[2:3, :] * lp[3:4, :], axis=1, keepdims=True)
    return jnp.exp(a) - jnp.exp(b) + lam_init


def _diff_prompt_kernel(lp_ref, gn_ref, q_ref, k_ref, v_ref, o_ref, m_scr, acc_scr,
                        *, tq, lam_init):
    i = pl.program_id(1)
    tk = tq
    nmap = 2 * DF_HEADS
    krow = lax.broadcasted_iota(I32, (tk, tq), 0)
    qcol = lax.broadcasted_iota(I32, (tk, tq), 1)
    causal = jnp.concatenate([krow <= qcol] * nmap, axis=1)
    q = q_ref[0] * (DF_DH ** -0.5 * math.log2(math.e))
    lane_map = lax.broadcasted_iota(I32, q.shape, 1) // DF_DH
    qbd = jnp.concatenate([jnp.where(lane_map == mi, q, 0.0) for mi in range(nmap)], axis=0).astype(BF16)
    m_scr[...] = jnp.full(m_scr.shape, -jnp.inf, F32)
    acc_scr[...] = jnp.zeros(acc_scr.shape, F32)

    def block(j, masked):
        off = pl.multiple_of(j * tk, tk)
        s = lax.dot_general(k_ref[0, pl.ds(off, tk), :], qbd, _NT, preferred_element_type=F32)
        if masked:
            s = jnp.where(causal, s, -jnp.inf)
        m_old = m_scr[...]
        m_new = jnp.maximum(m_old, jnp.max(s, axis=0, keepdims=True))
        alpha = jnp.exp2(m_old - m_new)
        pb = jnp.exp2(s - m_new).astype(BF16)
        m_scr[...] = m_new
        for h in range(DF_HEADS):
            cols = slice(2 * h * tq, (2 * h + 2) * tq)
            vh = v_ref[0, pl.ds(off, tk), h * LANES:(h + 1) * LANES]
            acc_scr[h] = alpha[:, cols] * acc_scr[h] + lax.dot_general(
                vh, pb[:, cols], _TN, preferred_element_type=F32)

    block(i, True)
    lax.fori_loop(0, i, lambda j, c: (block(j, False), c)[1], 0)

    lam = _diff_lambda(lp_ref, lam_init)
    for h in range(DF_HEADS):
        a = acc_scr[h]
        o = (a[:DF_VDH, :tq] / a[DF_VDH:DF_VDH + 1, :tq]
             - lam * (a[:DF_VDH, tq:] / a[DF_VDH:DF_VDH + 1, tq:])).T
        o = o * lax.rsqrt(jnp.mean(o * o, axis=1, keepdims=True) + EPS) * gn_ref[...]
        o_ref[0, :, h * DF_VDH:(h + 1) * DF_VDH] = o * (1.0 - lam_init)


def _diff_prompt(lp, gn, dq, dkb, dvb, lam_init, tq):
    b, t, w = dq.shape
    blk = pl.BlockSpec((1, tq, w), lambda bi, i: (bi, i, 0))
    full = lambda a: pl.BlockSpec((1, t, a.shape[2]), lambda bi, i: (bi, 0, 0))
    nmap = 2 * DF_HEADS
    return pl.pallas_call(
        functools.partial(_diff_prompt_kernel, tq=tq, lam_init=lam_init),
        grid=(b, t // tq),
        in_specs=[pl.BlockSpec(lp.shape, lambda bi, i: (0, 0)), pl.BlockSpec(gn.shape, lambda bi, i: (0, 0)),
                  blk, full(dkb), full(dvb)],
        out_specs=blk,
        out_shape=jax.ShapeDtypeStruct((b, t, w), F32),
        scratch_shapes=[pltpu.VMEM((1, nmap * tq), F32), pltpu.VMEM((DF_HEADS, LANES, 2 * tq), F32)],
        compiler_params=_cparams(("parallel", "parallel")),
        name="diff_prompt",
    )(lp, gn, dq, dkb, dvb)


def _out_ffn_kernel(x_ref, hm_ref, so_ref, hd_ref, wo_ref, gf_ref, wu_ref, wd_ref, gl_ref, y_ref,
                    xn_scr, *, final_norm):
    c = pl.program_id(1)

    @pl.when(c == 0)
    def _():
        x1 = (x_ref[...]
              + jnp.dot(hm_ref[...].astype(BF16), wo_ref[0:M_WIDTH, :], preferred_element_type=F32)
              + jnp.dot(so_ref[...].astype(BF16), wo_ref[M_WIDTH:M_WIDTH + S_WIDTH, :],
                        preferred_element_type=F32)
              + jnp.dot(hd_ref[...].astype(BF16), wo_ref[M_WIDTH + S_WIDTH:, :], preferred_element_type=F32))
        xn_scr[...] = (x1 * lax.rsqrt(jnp.mean(x1 * x1, axis=-1, keepdims=True) + EPS)
                       * gf_ref[...]).astype(BF16)
        y_ref[...] = x1

    u = jnp.maximum(jnp.dot(xn_scr[...], wu_ref[...], preferred_element_type=F32), 0.0)
    y_ref[...] += jnp.dot((u * u).astype(BF16), wd_ref[...], preferred_element_type=F32)

    if final_norm:
        @pl.when(c == pl.num_programs(1) - 1)
        def _():
            y = y_ref[...]
            y_ref[...] = y * lax.rsqrt(jnp.mean(y * y, axis=-1, keepdims=True) + EPS) * gl_ref[...]


def _out_ffn(x2d, hm, so, hd, wo, gf, wu, wd, gl, final_norm, tm, tf=1024):
    n, d = x2d.shape
    row = lambda w: pl.BlockSpec((tm, w), lambda i, c: (i, 0))
    const = lambda a: pl.BlockSpec(a.shape, lambda i, c: (0, 0))
    return pl.pallas_call(
        functools.partial(_out_ffn_kernel, final_norm=final_norm),
        grid=(n // tm, wu.shape[1] // tf),
        in_specs=[row(d), row(M_WIDTH), row(S_WIDTH), row(DF_WIDTH), const(wo), const(gf),
                  pl.BlockSpec((d, tf), lambda i, c: (0, c)), pl.BlockSpec((tf, d), lambda i, c: (c, 0)),
                  const(gl)],
        out_specs=row(d),
        out_shape=jax.ShapeDtypeStruct((n, d), F32),
        scratch_shapes=[pltpu.VMEM((tm, d), BF16)],
        compiler_params=_cparams(("parallel", "arbitrary")),
        name="out_ffn",
    )(x2d, hm, so, hd, wo, gf, wu, wd, gl)


def _page_specs(n, layer, width, page_rows, pages_per_step):
    return [pl.BlockSpec((1, 1, width, page_rows),
                         functools.partial(lambda b, c, pt, p: (layer, pt[b, c * pages_per_step + p], 0, 0), p=p))
            for p in range(n)]


def _seq_spec(shape):
    return pl.BlockSpec((1,) + shape, lambda b, c, pt: (b,) + (0,) * len(shape))


def _dsa_sample_score_kernel(pt_ref, iq_ref, iw_ref, ikn_ref, *rest, P, page, ds):
    pages = rest[:P]
    keys_ref, tkeys_ref = rest[P:]
    iq = iq_ref[0].astype(BF16)
    iw = iw_ref[0]

    def scores(d):
        r = jnp.maximum(d, 0.0) * iw
        return jnp.sum(r.reshape(ds, IDX_HEADS, d.shape[1]), axis=1)

    for p in range(P):
        d = jnp.dot(iq, pages[p][0, 0].astype(BF16), preferred_element_type=F32)
        keys_ref[0, :, p * page:(p + 1) * page] = _sort_key(scores(d))

    @pl.when(pl.program_id(1) == pl.num_programs(1) - 1)
    def _():
        trow = lax.broadcasted_iota(I32, (ds, page), 0)
        tcol = lax.broadcasted_iota(I32, (ds, page), 1)
        d = lax.dot_general(iq, ikn_ref[0].astype(BF16), _NT, preferred_element_type=F32)
        tkeys_ref[0] = jnp.where(tcol <= trow, _sort_key(scores(d)), jnp.int32(INT_MIN))


def _dsa_sample_score(pt, iq32, iw32, ik_new_pad, pool_ik, layer, ds, P):
    db, n_pages = pt.shape
    page = pool_ik.shape[3]
    past = n_pages * page
    out_shape = [jax.ShapeDtypeStruct((db, ds, past), I32), jax.ShapeDtypeStruct((db, ds, page), I32)]
    grid_spec = pltpu.PrefetchScalarGridSpec(
        num_scalar_prefetch=1, grid=(db, n_pages // P),
        in_specs=[_seq_spec(iq32.shape[1:]), _seq_spec(iw32.shape[1:]), _seq_spec(ik_new_pad.shape[1:])]
        + _page_specs(P, layer, IDX_DH, page, P),
        out_specs=[pl.BlockSpec((1, ds, P * page), lambda b, c, pt: (b, 0, c)), _seq_spec((ds, page))])
    return pl.pallas_call(
        functools.partial(_dsa_sample_score_kernel, P=P, page=page, ds=ds),
        grid_spec=grid_spec, out_shape=out_shape,
        compiler_params=_cparams(("parallel", "arbitrary")),
        name="dsa_sample_score",
    )(pt, iq32, iw32, ik_new_pad, *([pool_ik] * P))


def _dsa_sample_thresh_kernel(keys_ref, tkeys_ref, thr_ref, cut_ref, *, topk, nbits):
    keys, tkeys = keys_ref[...], tkeys_ref[...]
    past = keys.shape[1]
    idx = lax.broadcasted_iota(I32, keys.shape, 1)
    tidx = lax.broadcasted_iota(I32, tkeys.shape, 1) + past

    def cnt(pred):
        return (jnp.sum(jnp.where(pred(keys, idx), 1, 0).astype(I32), axis=1, keepdims=True)
                + jnp.sum(jnp.where(pred(tkeys, tidx), 1, 0).astype(I32), axis=1, keepdims=True))

    shape = (keys.shape[0], 1)
    thr = _kth_largest(lambda v: cnt(lambda k, i: k >= v), shape, topk)
    need = topk - cnt(lambda k, i: k > thr)
    cut = _index_cut(lambda t: cnt(lambda k, i: (k == thr) & (i < t)), shape, need, nbits)
    thr_ref[...] = jnp.broadcast_to(thr, thr_ref.shape)
    cut_ref[...] = jnp.broadcast_to(cut, cut_ref.shape)


def _dsa_sample_thresh(keys2d, tkeys2d, ds):
    r, past = keys2d.shape
    page = tkeys2d.shape[1]
    rb = _largest_tile(r, 32)
    topk = min(TOPK_MAX, (past + ds) // 4)
    nbits = max(1, (past + page - 1).bit_length())
    row = lambda w: pl.BlockSpec((rb, w), lambda i: (i, 0))
    return pl.pallas_call(
        functools.partial(_dsa_sample_thresh_kernel, topk=topk, nbits=nbits),
        grid=(r // rb,),
        in_specs=[row(past), row(page)],
        out_specs=[row(LANES), row(LANES)],
        out_shape=[jax.ShapeDtypeStruct((r, LANES), I32)] * 2,
        compiler_params=_cparams(("parallel",)),
        name="dsa_sample_thresh",
    )(keys2d, tkeys2d)


def _softmax_step(s, m_scr, l_scr):
    m_old = m_scr[...]
    m_new = jnp.maximum(m_old, jnp.max(s, axis=1, keepdims=True))
    m_safe = jnp.where(m_new == -jnp.inf, 0.0, m_new)
    alpha = jnp.exp(m_old - m_safe)
    p = jnp.exp(s - m_safe)
    l_scr[...] = alpha * l_scr[...] + jnp.sum(p, axis=1, keepdims=True)
    m_scr[...] = m_new
    return alpha, p


def _dsa_sample_attn_kernel(pt_ref, q_ref, keys_ref, tkeys_ref, thr_ref, cut_ref, kvn_ref, *rest,
                            P, page, ds):
    pages = rest[:P]
    o_ref, m_scr, l_scr, acc_scr = rest[P:]
    c = pl.program_id(1)
    nc = pl.num_programs(1)
    past = nc * P * page
    rows = ds * S_HEADS
    kw = S_KV_HEADS * S_DH

    @pl.when(c == 0)
    def _():
        m_scr[...] = jnp.full(m_scr.shape, -jnp.inf, F32)
        l_scr[...] = jnp.zeros(l_scr.shape, F32)
        acc_scr[...] = jnp.zeros(acc_scr.shape, F32)

    q = q_ref[0] * (S_DH ** -0.5)
    rr = lax.broadcasted_iota(I32, (rows, kw), 0)
    ll = lax.broadcasted_iota(I32, (rows, kw), 1)
    own = (ll // S_DH) == ((rr % S_HEADS) // S_REP)
    qbd = jnp.where(own, jnp.concatenate([q] * S_KV_HEADS, axis=1), 0.0).astype(BF16)
    thr = thr_ref[0][:, 0:1]
    cut = cut_ref[0][:, 0:1]

    def expand(sel):
        return jnp.broadcast_to(sel[:, None, :], (ds, S_HEADS, sel.shape[1])).reshape(rows, sel.shape[1])

    def attend(kv_blocks, sel, paged):
        def qk(kb):
            if paged:
                return jnp.dot(qbd, kb[:kw, :].astype(BF16), preferred_element_type=F32)
            return lax.dot_general(qbd, kb[:, :kw].astype(BF16), _NT, preferred_element_type=F32)

        def pv_of(pb, kb):
            if paged:
                return lax.dot_general(pb, kb[kw:, :].astype(BF16), _NT, preferred_element_type=F32)
            return jnp.dot(pb, kb[:, kw:].astype(BF16), preferred_element_type=F32)

        s = jnp.concatenate([qk(kb) for kb in kv_blocks], axis=1)
        s = jnp.where(expand(sel), s, -jnp.inf)
        alpha, p = _softmax_step(s, m_scr, l_scr)
        pv = jnp.zeros((rows, kw), F32)
        for bi, kb in enumerate(kv_blocks):
            pv = pv + pv_of(p[:, bi * page:(bi + 1) * page].astype(BF16), kb)
        acc_scr[...] = alpha * acc_scr[...] + pv

    keys = keys_ref[0]
    col = lax.broadcasted_iota(I32, keys.shape, 1) + c * (P * page)
    attend([pg[0, 0] for pg in pages], (keys > thr) | ((keys == thr) & (col <= cut)), True)

    @pl.when(c == nc - 1)
    def _():
        tkeys = tkeys_ref[0]
        trow = lax.broadcasted_iota(I32, tkeys.shape, 0)
        tcol = lax.broadcasted_iota(I32, tkeys.shape, 1)
        sel = ((tkeys > thr) | ((tkeys == thr) & (tcol + past <= cut))) & (tcol <= trow)
        attend([kvn_ref[0]], sel, False)
        o = acc_scr[...] / l_scr[...]
        first = ((lax.broadcasted_iota(I32, (rows, S_DH), 0) % S_HEADS) // S_REP) == 0
        o_ref[0] = jnp.where(first, o[:, :S_DH], o[:, S_DH:])


def _dsa_sample_attn(pt, q32, keys, tkeys, thr, cut, kv_new_pad, pool_kv, layer, ds, P):
    db, n_pages = pt.shape
    page = pool_kv.shape[3]
    nc = n_pages // P
    rows = ds * S_HEADS
    kw = S_KV_HEADS * S_DH
    grid_spec = pltpu.PrefetchScalarGridSpec(
        num_scalar_prefetch=1, grid=(db, nc),
        in_specs=[_seq_spec(q32.shape[1:]), pl.BlockSpec((1, ds, P * page), lambda b, c, pt: (b, 0, c)),
                  _seq_spec((ds, page)), _seq_spec((ds, LANES)), _seq_spec((ds, LANES)),
                  _seq_spec(kv_new_pad.shape[1:])] + _page_specs(P, layer, 2 * kw, page, P),
        out_specs=_seq_spec((rows, S_DH)),
        scratch_shapes=[pltpu.VMEM((rows, 1), F32), pltpu.VMEM((rows, 1), F32), pltpu.VMEM((rows, kw), F32)])
    return pl.pallas_call(
        functools.partial(_dsa_sample_attn_kernel, P=P, page=page, ds=ds),
        grid_spec=grid_spec, out_shape=jax.ShapeDtypeStruct((db, rows, S_DH), F32),
        compiler_params=_cparams(("parallel", "arbitrary")),
        name="dsa_sample_attn",
    )(pt, q32, keys, tkeys, thr, cut, kv_new_pad, *([pool_kv] * P))


def _diff_sample_kernel(pt_ref, lp_ref, gn_ref, q_ref, kn_ref, vn_ref, *rest, P, page, ds, lam_init):
    kpages, vpages = rest[:P], rest[P:2 * P]
    o_ref, m_scr, l_scr, acc_scr = rest[2 * P:]
    c = pl.program_id(1)
    nc = pl.num_programs(1)
    nmap = 2 * DF_HEADS
    rows = ds * nmap
    half = rows // 2

    @pl.when(c == 0)
    def _():
        m_scr[...] = jnp.full(m_scr.shape, -jnp.inf, F32)
        l_scr[...] = jnp.zeros(l_scr.shape, F32)
        acc_scr[...] = jnp.zeros(acc_scr.shape, F32)

    q = q_ref[0] * (DF_DH ** -0.5)
    rr = lax.broadcasted_iota(I32, (rows, DF_WIDTH), 0)
    ll = lax.broadcasted_iota(I32, (rows, DF_WIDTH), 1)
    own = (ll // DF_DH) == (2 * (rr % DF_HEADS) + rr // half)
    qbd = jnp.where(own, jnp.concatenate([q] * nmap, axis=1), 0.0).astype(BF16)

    def attend(kblocks, vblocks, mask, paged):
        if paged:
            s = jnp.concatenate([jnp.dot(qbd, kb.astype(BF16), preferred_element_type=F32)
                                 for kb in kblocks], axis=1)
        else:
            s = jnp.concatenate([lax.dot_general(qbd, kb.astype(BF16), _NT, preferred_element_type=F32)
                                 for kb in kblocks], axis=1)
        if mask is not None:
            s = jnp.where(mask, s, -jnp.inf)
        alpha, p = _softmax_step(s, m_scr, l_scr)
        pv = jnp.zeros((rows, DF_WIDTH), F32)
        for bi, vb in enumerate(vblocks):
            pb = p[:, bi * page:(bi + 1) * page].astype(BF16)
            if paged:
                pv = pv + lax.dot_general(pb, vb.astype(BF16), _NT, preferred_element_type=F32)
            else:
                pv = pv + jnp.dot(pb, vb.astype(BF16), preferred_element_type=F32)
        acc_scr[...] = alpha * acc_scr[...] + pv

    attend([pg[0, 0] for pg in kpages], [pg[0, 0] for pg in vpages], None, True)

    @pl.when(c == nc - 1)
    def _():
        trow = (lax.broadcasted_iota(I32, (rows, page), 0) % half) // DF_HEADS
        tcol = lax.broadcasted_iota(I32, (rows, page), 1)
        attend([kn_ref[0]], [vn_ref[0]], tcol <= trow, False)
        o = acc_scr[...] / l_scr[...]
        hrow = lax.broadcasted_iota(I32, (rows, DF_VDH), 0) % DF_HEADS
        oh = jnp.zeros((rows, DF_VDH), F32)
        for h in range(DF_HEADS):
            oh = jnp.where(hrow == h, o[:, h * DF_VDH:(h + 1) * DF_VDH], oh)
        lam = _diff_lambda(lp_ref, lam_init)
        d = oh[:half, :] - lam * oh[half:, :]
        d = d * lax.rsqrt(jnp.mean(d * d, axis=1, keepdims=True) + EPS) * gn_ref[...]
        o_ref[0] = d * (1.0 - lam_init)


def _diff_sample(pt, lp, gn, q32, k_new_pad, v_new_pad, pool_k, pool_v, layer, lam_init, ds, P):
    db, n_pages = pt.shape
    page = pool_k.shape[3]
    nc = n_pages // P
    rows = ds * 2 * DF_HEADS
    const = lambda a: pl.BlockSpec(a.shape, lambda b, c, pt: (0, 0))
    grid_spec = pltpu.PrefetchScalarGridSpec(
        num_scalar_prefetch=1, grid=(db, nc),
        in_specs=[const(lp), const(gn), _seq_spec(q32.shape[1:]), _seq_spec(k_new_pad.shape[1:]),
                  _seq_spec(v_new_pad.shape[1:])]
        + _page_specs(P, layer, DF_WIDTH, page, P) + _page_specs(P, layer, DF_WIDTH, page, P),
        out_specs=_seq_spec((rows // 2, DF_VDH)),
        scratch_shapes=[pltpu.VMEM((rows, 1), F32), pltpu.VMEM((rows, 1), F32),
                        pltpu.VMEM((rows, DF_WIDTH), F32)])
    return pl.pallas_call(
        functools.partial(_diff_sample_kernel, P=P, page=page, ds=ds, lam_init=lam_init),
        grid_spec=grid_spec, out_shape=jax.ShapeDtypeStruct((db, rows // 2, DF_VDH), F32),
        compiler_params=_cparams(("parallel", "arbitrary")),
        name="diff_sample",
    )(pt, lp, gn, q32, k_new_pad, v_new_pad, *([pool_k] * P), *([pool_v] * P))


def _largest_tile(n, cap):
    t = cap
    while n % t:
        t //= 2
    return t


def _pad_time(a, to, value=0.0):
    return jnp.pad(a, ((0, 0), (0, to - a.shape[1]), (0, 0)), constant_values=value)


def kernel(x_prompt, x_sample, cache_dsa_kv, cache_idx_k, cache_diff_k, cache_diff_v, state_mlstm_C, state_mlstm_n, state_mlstm_m, state_mlstm_conv, page_table, norm_mix, w_in, mlstm_conv_w, mlstm_conv_b, mlstm_gate_b, mlstm_norm, diff_lambda, diff_norm, w_out, norm_ffn, w_up, w_down, norm_final):
    B, T, D = x_prompt.shape
    DB, DS, _ = x_sample.shape
    depth = w_in.shape[0]
    n_pool, page = cache_idx_k.shape[1:3]
    past = page_table.shape[1] * page
    P = _largest_tile(page_table.shape[1], 32)
    LS = LANES

    tm_p = _largest_tile(T, 512)
    tabs_p = _rope_tables(jnp.arange(T), 1)
    tabs_s = _rope_tables(past + jnp.arange(DS), DB)
    gate_pad = jnp.zeros((LANES,), F32).at[G_IG:G_IG + M_HEADS].set(NEG_BIG)
    pool_ik = jnp.swapaxes(cache_idx_k, 2, 3)
    pool_kv = jnp.transpose(cache_dsa_kv, (0, 1, 3, 4, 5, 2)).reshape(depth, n_pool, -1, page)
    pool_dk = jnp.transpose(cache_diff_k, (0, 1, 3, 4, 5, 2)).reshape(depth, n_pool, -1, page)
    pool_dv = jnp.transpose(cache_diff_v, (0, 1, 3, 4, 2)).reshape(depth, n_pool, -1, page)

    xp = x_prompt.reshape(B * T, D)
    xs = x_sample.reshape(DB * DS, D)
    rec_p = [[] for _ in range(8)]
    rec_s = [[] for _ in range(8)]
    for l in range(depth):
        lam_init = 0.8 - 0.6 * math.exp(-0.3 * l)
        last = l == depth - 1
        w_perm = _permute_w_in(w_in[l])
        g_mix = norm_mix[l][None, :]
        gbias = jnp.zeros((1, LANES), F32).at[0, G_IG:G_IG + M_HEADS].set(mlstm_gate_b[l, 0]) \
            .at[0, G_LF:G_LF + M_HEADS].set(mlstm_gate_b[l, 1])
        conv_w, conv_b = mlstm_conv_w[l], mlstm_conv_b[l][None, :]
        nrm_m = mlstm_norm[l][None, :]
        lp, gn = diff_lambda[l], diff_norm[l][None, :]
        wo, wu, wd = w_out[l].astype(BF16), w_up[l].astype(BF16), w_down[l].astype(BF16)
        g_ffn, g_fin = norm_ffn[l][None, :], norm_final[None, :]

        (uqk, mv, mo, sq, skv, iq, dq, dk, dv, ik, gates, ikb, kvb, dkb, dvb) = [
            a.reshape(B, T, -1) for a in _inproj(xp, g_mix, w_perm, gbias, tabs_p, tm_p)]
        gates_t = jnp.swapaxes(gates[:, :, G_IG:G_IG + 16], 1, 2)
        hm, C, n, m = _mlstm(uqk, mv, mo, gates, gates_t, conv_w, conv_b, nrm_m,
                             jnp.zeros((B, CONV_W - 1, 2 * M_WIDTH), F32),
                             jnp.zeros((B, M_HEADS, M_DH, M_DH), F32), jnp.zeros((B, M_HEADS, 1, M_DH), F32),
                             jnp.zeros((B, M_HEADS, 1, 1), F32), _largest_tile(T, 256))
        s_out = _dsa_prompt(sq, iq, gates_t, ikb, kvb, _largest_tile(T, 256))
        hd = _diff_prompt(lp, gn, dq, dkb, dvb, lam_init, _largest_tile(T, 256))
        xp = _out_ffn(xp, hm.reshape(B * T, -1), s_out.reshape(B * T, -1), hd.reshape(B * T, -1),
                      wo, g_ffn, wu, wd, g_fin, last, tm_p)
        for lst, a in zip(rec_p, (skv.reshape(B, T, 2, S_KV_HEADS, S_DH), ik,
                                  dk.reshape(B, T, DF_HEADS, 2, DF_DH), dv.reshape(B, T, DF_HEADS, DF_VDH),
                                  C, n.reshape(B, M_HEADS, M_DH), m.reshape(B, M_HEADS),
                                  uqk[:, T - (CONV_W - 1):, :])):
            lst.append(a)

        (uqk, mv, mo, sq, skv, iq, dq, dk, dv, ik, gates) = [
            a.reshape(DB, DS, -1) for a in _inproj(xs, g_mix, w_perm, gbias, tabs_s, DB * DS)[:11]]
        gates_pd = jnp.concatenate([gates, jnp.broadcast_to(gate_pad, (DB, LS - DS, LANES))], axis=1)
        gates_t = jnp.swapaxes(gates_pd[:, :, G_IG:G_IG + 16], 1, 2)
        hm, C, n, m = _mlstm(_pad_time(uqk, LS), _pad_time(mv, LS), _pad_time(mo, LS), gates_pd, gates_t,
                             conv_w, conv_b, nrm_m, state_mlstm_conv[l], state_mlstm_C[l],
                             state_mlstm_n[l][:, :, None, :], state_mlstm_m[l][:, :, None, None], LS)
        hm = hm[:, :DS, :]
        keys, tkeys = _dsa_sample_score(
            page_table, iq.reshape(DB, DS * IDX_HEADS, IDX_DH),
            gates[:, :, G_IW:G_IW + IDX_HEADS].reshape(DB, DS * IDX_HEADS, 1),
            _pad_time(ik, page), pool_ik, l, DS, P)
        thr, cut = _dsa_sample_thresh(keys.reshape(DB * DS, -1), tkeys.reshape(DB * DS, -1), DS)
        s_out = _dsa_sample_attn(page_table, sq.reshape(DB, DS * S_HEADS, S_DH), keys, tkeys,
                                 thr.reshape(DB, DS, LANES), cut.reshape(DB, DS, LANES),
                                 _pad_time(skv, page), pool_kv, l, DS, P)
        dq32 = dq.reshape(DB, DS, DF_HEADS, 2, DF_DH).transpose(0, 3, 1, 2, 4).reshape(DB, -1, DF_DH)
        hd = _diff_sample(page_table, lp, gn, dq32, _pad_time(dk, page), _pad_time(dv, page),
                          pool_dk, pool_dv, l, lam_init, DS, P)
        xs = _out_ffn(xs, hm.reshape(DB * DS, -1), s_out.reshape(DB * DS, -1), hd.reshape(DB * DS, -1),
                      wo, g_ffn, wu, wd, g_fin, last, DB * DS)
        for lst, a in zip(rec_s, (skv.reshape(DB, DS, 2, S_KV_HEADS, S_DH), ik,
                                  dk.reshape(DB, DS, DF_HEADS, 2, DF_DH), dv.reshape(DB, DS, DF_HEADS, DF_VDH),
                                  C, n.reshape(DB, M_HEADS, M_DH), m.reshape(DB, M_HEADS),
                                  jnp.concatenate([state_mlstm_conv[l], uqk], axis=1)[:, DS:, :])):
            lst.append(a)

    outs_p = [jnp.stack(a) for a in rec_p]
    outs_s = [jnp.stack(a) for a in rec_s]
    return (xp.reshape(B, T, D), xs.reshape(DB, DS, D), *outs_p, *outs_s)
```

```python
import functools
import math

import jax
import jax.numpy as jnp
from jax import lax
from jax.experimental import pallas as pl
from jax.experimental.pallas import tpu as pltpu

F32 = jnp.float32
BF16 = jnp.bfloat16
I32 = jnp.int32

M_HEADS, M_DH = 4, 64
M_WIDTH = M_HEADS * M_DH
CONV_W = 4
S_HEADS, S_KV_HEADS, S_DH = 8, 2, 64
S_REP = S_HEADS // S_KV_HEADS
S_WIDTH = S_HEADS * S_DH
IDX_HEADS, IDX_DH = 8, 64
TOPK_MAX = 256
DF_HEADS, DF_DH = 4, 32
DF_VDH = 2 * DF_DH
DF_WIDTH = DF_HEADS * DF_VDH
ROPE_THETA = 10000.0
EPS = 1e-6

LANES = 128
VMEM_LIMIT = 56 * 1024 * 1024
INT_MIN = -(2 ** 31)
NEG_BIG = -1e30

C_UQK, C_MV, C_MO, C_SQ, C_SKV, C_IQ, C_DQ, C_DK, C_DV, C_MISC, C_END = (
    0, 512, 768, 1024, 1536, 1792, 2304, 2560, 2816, 3072, 3200)
G_IK, G_IG, G_LF, G_IW = 0, 64, 68, 72

_NT = (((1,), (1,)), ((), ()))
_TN = (((0,), (0,)), ((), ()))


def _cparams(sem):
    return pltpu.CompilerParams(dimension_semantics=sem, vmem_limit_bytes=VMEM_LIMIT)


def _swap_halves(zb, half, first):
    return jnp.where(first, pltpu.roll(zb, LANES - half, 1), pltpu.roll(zb, half, 1))


def _inproj_kernel(x_ref, g_ref, w_ref, gb_ref, c64_ref, s64_ref, c32_ref, s32_ref,
                   uqk_ref, mv_ref, mo_ref, sq_ref, skv_ref, iq_ref, dq_ref, dk_ref, dv_ref,
                   ik_ref, gates_ref, ikb_ref, kvb_ref, dkb_ref, dvb_ref):
    x = x_ref[...]
    ms = jnp.mean(x * x, axis=-1, keepdims=True)
    xn = (x * lax.rsqrt(ms + EPS) * g_ref[...]).astype(BF16)
    tm = x.shape[0]
    lane = lax.broadcasted_iota(I32, (tm, LANES), 1)
    first64 = (lane % 64) < 32
    first32 = (lane % 32) < 16
    c64, s64 = c64_ref[...], s64_ref[...]
    c32, s32 = c32_ref[...], s32_ref[...]

    def proj(a, b):
        return jnp.dot(xn, w_ref[:, a:b], preferred_element_type=F32)

    def rope_store(out_ref, z, nblk, c, s, half, first):
        for b in range(nblk):
            zb = z[:, b * LANES:(b + 1) * LANES]
            out_ref[:, b * LANES:(b + 1) * LANES] = zb * c + _swap_halves(zb, half, first) * s

    uqk_ref[...] = proj(C_UQK, C_MV)
    mv_ref[...] = proj(C_MV, C_MO)
    mo_ref[...] = jax.nn.sigmoid(proj(C_MO, C_SQ))
    rope_store(sq_ref, proj(C_SQ, C_SKV), 4, c64, s64, 32, first64)
    zkv = proj(C_SKV, C_IQ)
    rope_store(skv_ref, zkv, 1, c64, s64, 32, first64)
    skv_ref[:, LANES:] = zkv[:, LANES:]
    rope_store(iq_ref, proj(C_IQ, C_DQ), 4, c64, s64, 32, first64)
    rope_store(dq_ref, proj(C_DQ, C_DK), 2, c32, s32, 16, first32)
    rope_store(dk_ref, proj(C_DK, C_DV), 2, c32, s32, 16, first32)
    dv_ref[...] = proj(C_DV, C_MISC)
    zm = proj(C_MISC, C_END) + gb_ref[...]
    roped = zm * c64 + _swap_halves(zm, 32, first64) * s64
    logsig = jnp.minimum(zm, 0.0) - jnp.log1p(jnp.exp(-jnp.abs(zm)))
    is_lf = (lane >= G_LF) & (lane < G_IW)
    gates = jnp.where(lane < G_IG, roped, jnp.where(is_lf, logsig, zm))
    gates_ref[...] = gates
    ik_ref[...] = gates[:, :IDX_DH]

    low = lane < 64
    ikb_ref[...] = gates[:, :IDX_DH].astype(BF16)
    dkb_ref[...] = dk_ref[...].astype(BF16)
    kro, vkv = skv_ref[:, :LANES], zkv[:, LANES:]
    kvb_ref[:, 0 * LANES:1 * LANES] = jnp.where(low, kro, 0.0).astype(BF16)
    kvb_ref[:, 1 * LANES:2 * LANES] = jnp.where(low, pltpu.roll(kro, 64, 1), 0.0).astype(BF16)
    kvb_ref[:, 2 * LANES:3 * LANES] = jnp.where(low, vkv, 1.0).astype(BF16)
    kvb_ref[:, 3 * LANES:4 * LANES] = jnp.where(low, pltpu.roll(vkv, 64, 1), 1.0).astype(BF16)
    for b in range(DF_WIDTH // LANES):
        vb = dv_ref[:, b * LANES:(b + 1) * LANES]
        dvb_ref[:, (2 * b) * LANES:(2 * b + 1) * LANES] = jnp.where(low, vb, 1.0).astype(BF16)
        dvb_ref[:, (2 * b + 1) * LANES:(2 * b + 2) * LANES] = jnp.where(
            low, pltpu.roll(vb, 64, 1), 1.0).astype(BF16)


def _inproj(x2d, g, w_perm, gbias, tabs, tm):
    n, d = x2d.shape
    nt = tabs[0].shape[0] // tm
    row = lambda w: pl.BlockSpec((tm, w), lambda i: (i, 0))
    tab = pl.BlockSpec((tm, LANES), lambda i: (i % nt, 0))
    widths = (512, 256, 256, 512, 256, 512, 256, 256, 256, IDX_DH, LANES)
    widths_b = (IDX_DH, 4 * LANES, DF_WIDTH, 2 * DF_WIDTH)
    return pl.pallas_call(
        _inproj_kernel,
        grid=(n // tm,),
        in_specs=[row(d), pl.BlockSpec((1, d), lambda i: (0, 0)),
                  pl.BlockSpec((d, C_END), lambda i: (0, 0)),
                  pl.BlockSpec((1, LANES), lambda i: (0, 0)), tab, tab, tab, tab],
        out_specs=[row(w) for w in widths + widths_b],
        out_shape=[jax.ShapeDtypeStruct((n, w), F32) for w in widths]
        + [jax.ShapeDtypeStruct((n, w), BF16) for w in widths_b],
        compiler_params=_cparams(("parallel",)),
        name="inproj",
    )(x2d, g, w_perm, gbias, *tabs)


def _permute_w_in(w):
    d = w.shape[0]
    cols = [w[:, 0:1024], w[:, 1032:1544], w[:, 1544:1800], w[:, 1800:2312], w[:, 2384:3152],
            w[:, 2312:2376], w[:, 1024:1032], w[:, 2376:2384],
            jnp.zeros((d, C_END - 3152), w.dtype)]
    return jnp.concatenate(cols, axis=1).astype(BF16)


def _rope_tables(pos, reps):
    out = []
    for dh in (64, 32):
        inv = ROPE_THETA ** (-jnp.arange(0, dh, 2, dtype=F32) / dh)
        ang = pos.astype(F32)[:, None] * inv[None, :]
        cos, sin = jnp.cos(ang), jnp.sin(ang)
        c = jnp.tile(jnp.concatenate([cos, cos], axis=1), (reps, LANES // dh))
        s = jnp.tile(jnp.concatenate([-sin, sin], axis=1), (reps, LANES // dh))
        out += [c, s]
    return tuple(out)


def _mlstm_kernel(uqk_ref, mv_ref, mo_ref, gc_ref, gr_ref, cw_ref, cb_ref, nrm_ref,
                  cbuf_ref, c0_ref, n0_ref, m0_ref,
                  hm_ref, c_out_ref, n_out_ref, m_out_ref,
                  ext_scr, c_scr, n_scr, m_scr, *, L):
    @pl.when(pl.program_id(1) == 0)
    def _():
        ext_scr[5:8, :] = cbuf_ref[0]
        c_scr[...] = c0_ref[0]
        n_scr[...] = n0_ref[0]
        m_scr[...] = m0_ref[0]

    u = uqk_ref[0]
    ext_scr[8:8 + L, :] = u
    y = (cb_ref[...] + ext_scr[5:5 + L, :] * cw_ref[0:1, :] + ext_scr[6:6 + L, :] * cw_ref[1:2, :]
         + ext_scr[7:7 + L, :] * cw_ref[2:3, :] + u * cw_ref[3:4, :])
    ext_scr[5:8, :] = ext_scr[L + 5:L + 8, :]
    qk = y * jax.nn.sigmoid(y)
    q_all = qk[:, :M_WIDTH]
    k_all = qk[:, M_WIDTH:] * (M_DH ** -0.5)
    v_all = mv_ref[0]
    o_all = mo_ref[0]
    gc = gc_ref[0]
    gr = gr_ref[0]

    row = lax.broadcasted_iota(I32, (L, L), 0)
    col = lax.broadcasted_iota(I32, (L, L), 1)
    tril = row >= col
    tri = tril.astype(F32)
    bc_all = jnp.dot(tri, gc, precision=lax.Precision.HIGHEST, preferred_element_type=F32)
    br_all = lax.dot_general(gr, tri, _NT, precision=lax.Precision.HIGHEST, preferred_element_type=F32)

    for h in range(M_HEADS):
        sl = slice(h * M_DH, (h + 1) * M_DH)
        qh, kh, vh = q_all[:, sl], k_all[:, sl], v_all[:, sl]
        b_col = bc_all[:, G_LF + h:G_LF + h + 1]
        ig_col = gc[:, G_IG + h:G_IG + h + 1]
        b_row = br_all[M_HEADS + h:M_HEADS + h + 1, :]
        ig_row = gr[h:h + 1, :]
        m_prev = m_scr[h]
        c_prev = c_scr[h]
        n_prev = n_scr[h]

        a_col = b_col + m_prev
        dmat = jnp.where(tril, b_col - b_row + ig_row, -jnp.inf)
        m_t = jnp.maximum(a_col, jnp.max(dmat, axis=1, keepdims=True))
        w_inter = jnp.exp(a_col - m_t)
        w_intra = jnp.exp(dmat - m_t)
        qb, kb = qh.astype(BF16), kh.astype(BF16)
        s = lax.dot_general(qb, kb, _NT, preferred_element_type=F32)
        qkw = s * w_intra
        inter = lax.dot_general(qb, c_prev.astype(BF16), _NT, preferred_element_type=F32)
        num = jnp.dot(qkw.astype(BF16), vh.astype(BF16), preferred_element_type=F32) + w_inter * inter
        den = (jnp.sum(qkw, axis=1, keepdims=True)
               + w_inter * jnp.sum(qh * n_prev, axis=1, keepdims=True))
        hh = num / jnp.maximum(jnp.abs(den), jnp.exp(-m_t))

        m_last = m_t[L - 1:L, :]
        wl_inter = w_inter[L - 1:L, :]
        wl_col = jnp.exp(b_col[L - 1:L, :] - b_col + ig_col - m_last)
        vw = (vh * wl_col).astype(BF16)
        c_new = wl_inter * c_prev + lax.dot_general(vw, kb, _TN, preferred_element_type=F32)
        n_new = wl_inter * n_prev + jnp.sum(kh * wl_col, axis=0, keepdims=True)
        c_scr[h] = c_new
        n_scr[h] = n_new
        m_scr[h] = m_last
        c_out_ref[0, h] = c_new
        n_out_ref[0, h] = n_new
        m_out_ref[0, h] = m_last

        hn = hh * lax.rsqrt(jnp.mean(hh * hh, axis=1, keepdims=True) + EPS) * nrm_ref[:, sl]
        hm_ref[0, :, sl] = hn * o_all[:, sl]


def _mlstm(uqk, mv, mo, gates, gates_t, conv_w, conv_b, nrm, cbuf, c0, n0, m0, L):
    b, t, _ = uqk.shape
    tok = lambda w: pl.BlockSpec((1, L, w), lambda i, c: (i, c, 0))
    const2 = lambda a: pl.BlockSpec(a.shape, lambda i, c: (0, 0))
    per_b = lambda a: pl.BlockSpec((1,) + a.shape[1:], lambda i, c: (i,) + (0,) * (a.ndim - 1))
    out_shape = [jax.ShapeDtypeStruct((b, t, M_WIDTH), F32),
                 jax.ShapeDtypeStruct((b, M_HEADS, M_DH, M_DH), F32),
                 jax.ShapeDtypeStruct((b, M_HEADS, 1, M_DH), F32),
                 jax.ShapeDtypeStruct((b, M_HEADS, 1, 1), F32)]
    return pl.pallas_call(
        functools.partial(_mlstm_kernel, L=L),
        grid=(b, t // L),
        in_specs=[tok(2 * M_WIDTH), tok(M_WIDTH), tok(M_WIDTH), tok(LANES),
                  pl.BlockSpec((1, 16, L), lambda i, c: (i, 0, c)),
                  const2(conv_w), const2(conv_b), const2(nrm),
                  per_b(cbuf), per_b(c0), per_b(n0), per_b(m0)],
        out_specs=[tok(M_WIDTH)] + [pl.BlockSpec((1,) + s.shape[1:], lambda i, c: (i, 0, 0, 0))
                                    for s in out_shape[1:]],
        out_shape=out_shape,
        scratch_shapes=[pltpu.VMEM((L + 8, 2 * M_WIDTH), F32),
                        pltpu.VMEM((M_HEADS, M_DH, M_DH), F32),
                        pltpu.VMEM((M_HEADS, 1, M_DH), F32),
                        pltpu.VMEM((M_HEADS, 1, 1), F32)],
        compiler_params=_cparams(("parallel", "arbitrary")),
        name="mlstm",
    )(uqk, mv, mo, gates, gates_t, conv_w, conv_b, nrm, cbuf, c0, n0, m0)


def _sort_key(score):
    bits = pltpu.bitcast(score + 0.0, I32)
    return jnp.where(bits >= 0, bits, bits ^ jnp.int32(0x7FFFFFFF))


def _kth_largest(count_ge, shape, k):
    def body(it, ans):
        bit = jnp.left_shift(jnp.int32(1), 31 - it)
        cand = ans | bit
        cnt = count_ge(cand ^ jnp.int32(INT_MIN))
        return jnp.where(cnt >= k, cand, ans)
    ans = lax.fori_loop(0, 32, body, jnp.zeros(shape, I32))
    return ans ^ jnp.int32(INT_MIN)


def _index_cut(count_eq_below, shape, need, nbits):
    def body(it, p):
        t = p | jnp.left_shift(jnp.int32(1), nbits - 1 - it)
        return jnp.where(count_eq_below(t) < need, t, p)
    return lax.fori_loop(0, nbits, body, jnp.zeros(shape, I32))


def _dsa_prompt_kernel(sq_ref, iq_ref, iwt_ref, ik_ref, kv_ref, o_ref,
                       key_scr, m_scr, acc_scr, *, tq, topk, nbits):
    i = pl.program_id(1)
    nblk = i + 1
    tk = tq
    krow = lax.broadcasted_iota(I32, (tk, tq), 0)
    qcol = lax.broadcasted_iota(I32, (tk, tq), 1)
    causal = krow <= qcol

    iq = iq_ref[0].astype(BF16)
    iq_h = [iq[:, h * IDX_DH:(h + 1) * IDX_DH] for h in range(IDX_HEADS)]
    iw = iwt_ref[0]

    def score_block(j, masked):
        off = pl.multiple_of(j * tk, tk)
        ikb = ik_ref[0, pl.ds(off, tk), :]
        sc = jnp.zeros((tk, tq), F32)
        for h in range(IDX_HEADS):
            d = lax.dot_general(ikb, iq_h[h], _NT, preferred_element_type=F32)
            sc = sc + jnp.maximum(d, 0.0) * iw[h:h + 1, :]
        key = _sort_key(sc)
        if masked:
            key = jnp.where(causal, key, jnp.int32(INT_MIN))
        key_scr[pl.ds(off, tk), :] = key

    lax.fori_loop(0, nblk - 1, lambda j, c: (score_block(j, False), c)[1], 0)
    score_block(nblk - 1, True)

    fold = 32

    def count(pred):
        def part(j):
            kb = key_scr[pl.ds(pl.multiple_of(j * tk, tk), tk), :]
            ones = jnp.where(pred(kb, j), 1, 0).astype(I32)
            return jnp.sum(ones.reshape(tk // fold, fold, tq), axis=0)
        acc = lax.fori_loop(0, nblk // 2, lambda j, a: a + part(2 * j) + part(2 * j + 1),
                            jnp.zeros((fold, tq), I32))
        acc = lax.cond(nblk % 2 == 1, lambda: acc + part(nblk - 1), lambda: acc)
        return jnp.sum(acc, axis=0, keepdims=True)

    thr = _kth_largest(lambda v: count(lambda kb, j: kb >= v), (1, tq), topk)
    need = topk - count(lambda kb, j: kb > thr)
    n_eq = count(lambda kb, j: kb == thr)
    tied = jnp.max(jnp.where(n_eq > need, 1, 0).astype(I32))
    cut = lax.cond(
        tied > 0,
        lambda: _index_cut(lambda t: count(lambda kb, j: (kb == thr) & (j * tk + krow < t)),
                           (1, tq), need, nbits),
        lambda: jnp.full((1, tq), (1 << nbits) - 1, I32))

    sq = sq_ref[0] * (S_DH ** -0.5 * math.log2(math.e))
    zpad = jnp.zeros((tq, LANES - S_DH), F32)
    q_st = [jnp.concatenate([jnp.concatenate([sq[:, (n * S_REP + r) * S_DH:(n * S_REP + r + 1) * S_DH], zpad],
                                             axis=1) for r in range(S_REP)], axis=0).astype(BF16)
            for n in range(S_KV_HEADS)]
    m_scr[...] = jnp.full(m_scr.shape, -jnp.inf, F32)
    acc_scr[...] = jnp.zeros(acc_scr.shape, F32)

    def attend_blocks(blocks):
        offs = [pl.multiple_of(j * tk, tk) for j, _ in blocks]
        raw = [[lax.dot_general(kv_ref[0, pl.ds(off, tk), n * LANES:(n + 1) * LANES], q_st[n], _NT,
                                preferred_element_type=F32) for n in range(S_KV_HEADS)]
               for off in offs]
        for (j, masked), off, raw_j in zip(blocks, offs, raw):
            kb = key_scr[pl.ds(off, tk), :]
            sel = (kb > thr) | ((kb == thr) & (j * tk + krow <= cut))
            if masked:
                sel = sel & causal
            sel = jnp.concatenate([sel] * S_REP, axis=1)
            for n in range(S_KV_HEADS):
                vn = kv_ref[0, pl.ds(off, tk), (S_KV_HEADS + n) * LANES:(S_KV_HEADS + n + 1) * LANES]
                s = jnp.where(sel, raw_j[n], -jnp.inf)
                m_old = m_scr[n]
                m_new = jnp.maximum(m_old, jnp.max(s, axis=0, keepdims=True))
                m_safe = jnp.where(m_new == -jnp.inf, 0.0, m_new)
                alpha = jnp.exp2(m_old - m_safe)
                p = jnp.exp2(s - m_safe).astype(BF16)
                acc_scr[n] = alpha * acc_scr[n] + lax.dot_general(vn, p, _TN, preferred_element_type=F32)
                m_scr[n] = m_new

    lax.fori_loop(0, (nblk - 1) // 2,
                  lambda j, c: (attend_blocks([(2 * j, False), (2 * j + 1, False)]), c)[1], 0)
    lax.cond((nblk - 1) % 2 == 1,
             lambda: attend_blocks([(nblk - 2, False), (nblk - 1, True)]),
             lambda: attend_blocks([(nblk - 1, True)]))

    for n in range(S_KV_HEADS):
        a = acc_scr[n]
        o_t = a[:S_DH, :] / a[S_DH:S_DH + 1, :]
        for r in range(S_REP):
            h = n * S_REP + r
            o_ref[0, :, h * S_DH:(h + 1) * S_DH] = o_t[:, r * tq:(r + 1) * tq].T


def _dsa_prompt(sq, iq, gates_t, ikb, kvb, tq):
    b, t, _ = sq.shape
    topk = min(TOPK_MAX, t // 4)
    nbits = max(1, (t - 1).bit_length())
    blk = lambda w: pl.BlockSpec((1, tq, w), lambda bi, i: (bi, i, 0))
    full = lambda w: pl.BlockSpec((1, t, w), lambda bi, i: (bi, 0, 0))
    return pl.pallas_call(
        functools.partial(_dsa_prompt_kernel, tq=tq, topk=topk, nbits=nbits),
        grid=(b, t // tq),
        in_specs=[blk(S_WIDTH), blk(IDX_HEADS * IDX_DH),
                  pl.BlockSpec((1, IDX_HEADS, tq), lambda bi, i: (bi, 1, i)),
                  full(IDX_DH), full(kvb.shape[2])],
        out_specs=blk(S_WIDTH),
        out_shape=jax.ShapeDtypeStruct((b, t, S_WIDTH), F32),
        scratch_shapes=[pltpu.VMEM((t, tq), I32),
                        pltpu.VMEM((S_KV_HEADS, 1, S_REP * tq), F32),
                        pltpu.VMEM((S_KV_HEADS, LANES, S_REP * tq), F32)],
        compiler_params=_cparams(("parallel", "parallel")),
        name="dsa_prompt",
    )(sq, iq, gates_t, ikb, kvb)


def _diff_lambda(lp_ref, lam_init):
    lp = lp_ref[...]
    a = jnp.sum(lp[0:1, :] * lp[1:2, :], axis=1, keepdims=True)
    b = jnp.sum(lp[2:3, :] * lp[3:4, :], axis=1, keepdims=True)
    return jnp.exp(a) - jnp.exp(b) + lam_init


def _diff_prompt_kernel(lp_ref, gn_ref, q_ref, k_ref, v_ref, o_ref, m_scr, acc_scr,
                        *, tq, lam_init):
    i = pl.program_id(1)
    tk = tq
    nmap = 2 * DF_HEADS
    krow = lax.broadcasted_iota(I32, (tk, tq), 0)
    qcol = lax.broadcasted_iota(I32, (tk, tq), 1)
    causal = jnp.concatenate([krow <= qcol] * nmap, axis=1)
    q = q_ref[0] * (DF_DH ** -0.5 * math.log2(math.e))
    lane_map = lax.broadcasted_iota(I32, q.shape, 1) // DF_DH
    qbd = jnp.concatenate([jnp.where(lane_map == mi, q, 0.0) for mi in range(nmap)], axis=0).astype(BF16)
    m_scr[...] = jnp.full(m_scr.shape, -jnp.inf, F32)
    acc_scr[...] = jnp.zeros(acc_scr.shape, F32)

    def attend_blocks(blocks):
        offs = [pl.multiple_of(j * tk, tk) for j, _ in blocks]
        raw = [[lax.dot_general(k_ref[0, pl.ds(off, tk), :], qbd[2 * h * tq:(2 * h + 2) * tq, :], _NT,
                                preferred_element_type=F32) for h in range(DF_HEADS)]
               for off in offs]
        for (j, masked), off, raw_j in zip(blocks, offs, raw):
            for h in range(DF_HEADS):
                cols = slice(2 * h * tq, (2 * h + 2) * tq)
                s = jnp.where(causal[:, :2 * tq], raw_j[h], -jnp.inf) if masked else raw_j[h]
                m_old = m_scr[:, cols]
                m_new = jnp.maximum(m_old, jnp.max(s, axis=0, keepdims=True))
                alpha = jnp.exp2(m_old - m_new)
                pb = jnp.exp2(s - m_new).astype(BF16)
                m_scr[:, cols] = m_new
                vh = v_ref[0, pl.ds(off, tk), h * LANES:(h + 1) * LANES]
                acc_scr[h] = alpha * acc_scr[h] + lax.dot_general(vh, pb, _TN, preferred_element_type=F32)

    odd = i % 2
    lax.cond(odd == 1,
             lambda: attend_blocks([(i, True), (i - 1, False)]),
             lambda: attend_blocks([(i, True)]))
    lax.fori_loop(0, i // 2,
                  lambda j, c: (attend_blocks([(2 * j, False), (2 * j + 1, False)]), c)[1], 0)

    lam = _diff_lambda(lp_ref, lam_init)
    for h in range(DF_HEADS):
        a = acc_scr[h]
        o = (a[:DF_VDH, :tq] / a[DF_VDH:DF_VDH + 1, :tq]
             - lam * (a[:DF_VDH, tq:] / a[DF_VDH:DF_VDH + 1, tq:])).T
        o = o * lax.rsqrt(jnp.mean(o * o, axis=1, keepdims=True) + EPS) * gn_ref[...]
        o_ref[0, :, h * DF_VDH:(h + 1) * DF_VDH] = o * (1.0 - lam_init)


def _diff_prompt(lp, gn, dq, dkb, dvb, lam_init, tq):
    b, t, w = dq.shape
    blk = pl.BlockSpec((1, tq, w), lambda bi, i: (bi, i, 0))
    full = lambda a: pl.BlockSpec((1, t, a.shape[2]), lambda bi, i: (bi, 0, 0))
    nmap = 2 * DF_HEADS
    return pl.pallas_call(
        functools.partial(_diff_prompt_kernel, tq=tq, lam_init=lam_init),
        grid=(b, t // tq),
        in_specs=[pl.BlockSpec(lp.shape, lambda bi, i: (0, 0)), pl.BlockSpec(gn.shape, lambda bi, i: (0, 0)),
                  blk, full(dkb), full(dvb)],
        out_specs=blk,
        out_shape=jax.ShapeDtypeStruct((b, t, w), F32),
        scratch_shapes=[pltpu.VMEM((1, nmap * tq), F32), pltpu.VMEM((DF_HEADS, LANES, 2 * tq), F32)],
        compiler_params=_cparams(("parallel", "parallel")),
        name="diff_prompt",
    )(lp, gn, dq, dkb, dvb)


def _out_ffn_kernel(x_ref, hm_ref, so_ref, hd_ref, wo_ref, gf_ref, wu_ref, wd_ref, gl_ref, y_ref,
                    xn_scr, *, final_norm):
    c = pl.program_id(1)

    @pl.when(c == 0)
    def _():
        x1 = (x_ref[...]
              + jnp.dot(hm_ref[...].astype(BF16), wo_ref[0:M_WIDTH, :], preferred_element_type=F32)
              + jnp.dot(so_ref[...].astype(BF16), wo_ref[M_WIDTH:M_WIDTH + S_WIDTH, :],
                        preferred_element_type=F32)
              + jnp.dot(hd_ref[...].astype(BF16), wo_ref[M_WIDTH + S_WIDTH:, :], preferred_element_type=F32))
        xn_scr[...] = (x1 * lax.rsqrt(jnp.mean(x1 * x1, axis=-1, keepdims=True) + EPS)
                       * gf_ref[...]).astype(BF16)
        y_ref[...] = x1

    u = jnp.maximum(jnp.dot(xn_scr[...], wu_ref[...], preferred_element_type=F32), 0.0)
    y_ref[...] += jnp.dot((u * u).astype(BF16), wd_ref[...], preferred_element_type=F32)

    if final_norm:
        @pl.when(c == pl.num_programs(1) - 1)
        def _():
            y = y_ref[...]
            y_ref[...] = y * lax.rsqrt(jnp.mean(y * y, axis=-1, keepdims=True) + EPS) * gl_ref[...]


def _out_ffn(x2d, hm, so, hd, wo, gf, wu, wd, gl, final_norm, tm, tf=1024):
    n, d = x2d.shape
    row = lambda w: pl.BlockSpec((tm, w), lambda i, c: (i, 0))
    const = lambda a: pl.BlockSpec(a.shape, lambda i, c: (0, 0))
    return pl.pallas_call(
        functools.partial(_out_ffn_kernel, final_norm=final_norm),
        grid=(n // tm, wu.shape[1] // tf),
        in_specs=[row(d), row(M_WIDTH), row(S_WIDTH), row(DF_WIDTH), const(wo), const(gf),
                  pl.BlockSpec((d, tf), lambda i, c: (0, c)), pl.BlockSpec((tf, d), lambda i, c: (c, 0)),
                  const(gl)],
        out_specs=row(d),
        out_shape=jax.ShapeDtypeStruct((n, d), F32),
        scratch_shapes=[pltpu.VMEM((tm, d), BF16)],
        compiler_params=_cparams(("parallel", "arbitrary")),
        name="out_ffn",
    )(x2d, hm, so, hd, wo, gf, wu, wd, gl)


def _page_specs(n, layer, width, page_rows, pages_per_step):
    return [pl.BlockSpec((1, 1, width, page_rows),
                         functools.partial(lambda b, c, pt, p: (layer, pt[b, c * pages_per_step + p], 0, 0), p=p))
            for p in range(n)]


def _seq_spec(shape):
    return pl.BlockSpec((1,) + shape, lambda b, c, pt: (b,) + (0,) * len(shape))


def _dsa_sample_score_kernel(pt_ref, iq_ref, iw_ref, ikn_ref, *rest, P, page, ds):
    pages = rest[:P]
    keys_ref, tkeys_ref = rest[P:]
    iq = iq_ref[0].astype(BF16)
    iw = iw_ref[0]

    def scores(d):
        r = jnp.maximum(d, 0.0) * iw
        return jnp.sum(r.reshape(ds, IDX_HEADS, d.shape[1]), axis=1)

    for p in range(P):
        d = jnp.dot(iq, pages[p][0, 0].astype(BF16), preferred_element_type=F32)
        keys_ref[0, :, p * page:(p + 1) * page] = _sort_key(scores(d))

    @pl.when(pl.program_id(1) == pl.num_programs(1) - 1)
    def _():
        trow = lax.broadcasted_iota(I32, (ds, page), 0)
        tcol = lax.broadcasted_iota(I32, (ds, page), 1)
        d = lax.dot_general(iq, ikn_ref[0].astype(BF16), _NT, preferred_element_type=F32)
        tkeys_ref[0] = jnp.where(tcol <= trow, _sort_key(scores(d)), jnp.int32(INT_MIN))


def _dsa_sample_score(pt, iq32, iw32, ik_new_pad, pool_ik, layer, ds, P):
    db, n_pages = pt.shape
    page = pool_ik.shape[3]
    past = n_pages * page
    out_shape = [jax.ShapeDtypeStruct((db, ds, past), I32), jax.ShapeDtypeStruct((db, ds, page), I32)]
    grid_spec = pltpu.PrefetchScalarGridSpec(
        num_scalar_prefetch=1, grid=(db, n_pages // P),
        in_specs=[_seq_spec(iq32.shape[1:]), _seq_spec(iw32.shape[1:]), _seq_spec(ik_new_pad.shape[1:])]
        + _page_specs(P, layer, IDX_DH, page, P),
        out_specs=[pl.BlockSpec((1, ds, P * page), lambda b, c, pt: (b, 0, c)), _seq_spec((ds, page))])
    return pl.pallas_call(
        functools.partial(_dsa_sample_score_kernel, P=P, page=page, ds=ds),
        grid_spec=grid_spec, out_shape=out_shape,
        compiler_params=_cparams(("parallel", "arbitrary")),
        name="dsa_sample_score",
    )(pt, iq32, iw32, ik_new_pad, *([pool_ik] * P))


def _dsa_sample_thresh_kernel(keys_ref, tkeys_ref, thr_ref, cut_ref, *, topk, nbits):
    keys, tkeys = keys_ref[...], tkeys_ref[...]
    past = keys.shape[1]
    idx = lax.broadcasted_iota(I32, keys.shape, 1)
    tidx = lax.broadcasted_iota(I32, tkeys.shape, 1) + past

    def cnt(pred):
        return (jnp.sum(jnp.where(pred(keys, idx), 1, 0).astype(I32), axis=1, keepdims=True)
                + jnp.sum(jnp.where(pred(tkeys, tidx), 1, 0).astype(I32), axis=1, keepdims=True))

    shape = (keys.shape[0], 1)
    thr = _kth_largest(lambda v: cnt(lambda k, i: k >= v), shape, topk)
    need = topk - cnt(lambda k, i: k > thr)
    cut = _index_cut(lambda t: cnt(lambda k, i: (k == thr) & (i < t)), shape, need, nbits)
    thr_ref[...] = jnp.broadcast_to(thr, thr_ref.shape)
    cut_ref[...] = jnp.broadcast_to(cut, cut_ref.shape)


def _dsa_sample_thresh(keys2d, tkeys2d, ds):
    r, past = keys2d.shape
    page = tkeys2d.shape[1]
    rb = _largest_tile(r, 32)
    topk = min(TOPK_MAX, (past + ds) // 4)
    nbits = max(1, (past + page - 1).bit_length())
    row = lambda w: pl.BlockSpec((rb, w), lambda i: (i, 0))
    return pl.pallas_call(
        functools.partial(_dsa_sample_thresh_kernel, topk=topk, nbits=nbits),
        grid=(r // rb,),
        in_specs=[row(past), row(page)],
        out_specs=[row(LANES), row(LANES)],
        out_shape=[jax.ShapeDtypeStruct((r, LANES), I32)] * 2,
        compiler_params=_cparams(("parallel",)),
        name="dsa_sample_thresh",
    )(keys2d, tkeys2d)


def _softmax_step(s, m_scr, l_scr):
    m_old = m_scr[...]
    m_new = jnp.maximum(m_old, jnp.max(s, axis=1, keepdims=True))
    m_safe = jnp.where(m_new == -jnp.inf, 0.0, m_new)
    alpha = jnp.exp(m_old - m_safe)
    p = jnp.exp(s - m_safe)
    l_scr[...] = alpha * l_scr[...] + jnp.sum(p, axis=1, keepdims=True)
    m_scr[...] = m_new
    return alpha, p


def _dsa_sample_attn_kernel(pt_ref, q_ref, keys_ref, tkeys_ref, thr_ref, cut_ref, kvn_ref, *rest,
                            P, page, ds):
    pages = rest[:P]
    o_ref, m_scr, l_scr, acc_scr = rest[P:]
    c = pl.program_id(1)
    nc = pl.num_programs(1)
    past = nc * P * page
    rows = ds * S_HEADS
    kw = S_KV_HEADS * S_DH

    @pl.when(c == 0)
    def _():
        m_scr[...] = jnp.full(m_scr.shape, -jnp.inf, F32)
        l_scr[...] = jnp.zeros(l_scr.shape, F32)
        acc_scr[...] = jnp.zeros(acc_scr.shape, F32)

    q = q_ref[0] * (S_DH ** -0.5)
    rr = lax.broadcasted_iota(I32, (rows, kw), 0)
    ll = lax.broadcasted_iota(I32, (rows, kw), 1)
    own = (ll // S_DH) == ((rr % S_HEADS) // S_REP)
    qbd = jnp.where(own, jnp.concatenate([q] * S_KV_HEADS, axis=1), 0.0).astype(BF16)
    thr = thr_ref[0][:, 0:1]
    cut = cut_ref[0][:, 0:1]

    def expand(sel):
        return jnp.broadcast_to(sel[:, None, :], (ds, S_HEADS, sel.shape[1])).reshape(rows, sel.shape[1])

    def attend(kv_blocks, sel, paged):
        def qk(kb):
            if paged:
                return jnp.dot(qbd, kb[:kw, :].astype(BF16), preferred_element_type=F32)
            return lax.dot_general(qbd, kb[:, :kw].astype(BF16), _NT, preferred_element_type=F32)

        def pv_of(pb, kb):
            if paged:
                return lax.dot_general(pb, kb[kw:, :].astype(BF16), _NT, preferred_element_type=F32)
            return jnp.dot(pb, kb[:, kw:].astype(BF16), preferred_element_type=F32)

        s = jnp.concatenate([qk(kb) for kb in kv_blocks], axis=1)
        s = jnp.where(expand(sel), s, -jnp.inf)
        alpha, p = _softmax_step(s, m_scr, l_scr)
        pv = jnp.zeros((rows, kw), F32)
        for bi, kb in enumerate(kv_blocks):
            pv = pv + pv_of(p[:, bi * page:(bi + 1) * page].astype(BF16), kb)
        acc_scr[...] = alpha * acc_scr[...] + pv

    keys = keys_ref[0]
    col = lax.broadcasted_iota(I32, keys.shape, 1) + c * (P * page)
    attend([pg[0, 0] for pg in pages], (keys > thr) | ((keys == thr) & (col <= cut)), True)

    @pl.when(c == nc - 1)
    def _():
        tkeys = tkeys_ref[0]
        trow = lax.broadcasted_iota(I32, tkeys.shape, 0)
        tcol = lax.broadcasted_iota(I32, tkeys.shape, 1)
        sel = ((tkeys > thr) | ((tkeys == thr) & (tcol + past <= cut))) & (tcol <= trow)
        attend([kvn_ref[0]], sel, False)
        o = acc_scr[...] / l_scr[...]
        first = ((lax.broadcasted_iota(I32, (rows, S_DH), 0) % S_HEADS) // S_REP) == 0
        o_ref[0] = jnp.where(first, o[:, :S_DH], o[:, S_DH:])


def _dsa_sample_attn(pt, q32, keys, tkeys, thr, cut, kv_new_pad, pool_kv, layer, ds, P):
    db, n_pages = pt.shape
    page = pool_kv.shape[3]
    nc = n_pages // P
    rows = ds * S_HEADS
    kw = S_KV_HEADS * S_DH
    grid_spec = pltpu.PrefetchScalarGridSpec(
        num_scalar_prefetch=1, grid=(db, nc),
        in_specs=[_seq_spec(q32.shape[1:]), pl.BlockSpec((1, ds, P * page), lambda b, c, pt: (b, 0, c)),
                  _seq_spec((ds, page)), _seq_spec((ds, LANES)), _seq_spec((ds, LANES)),
                  _seq_spec(kv_new_pad.shape[1:])] + _page_specs(P, layer, 2 * kw, page, P),
        out_specs=_seq_spec((rows, S_DH)),
        scratch_shapes=[pltpu.VMEM((rows, 1), F32), pltpu.VMEM((rows, 1), F32), pltpu.VMEM((rows, kw), F32)])
    return pl.pallas_call(
        functools.partial(_dsa_sample_attn_kernel, P=P, page=page, ds=ds),
        grid_spec=grid_spec, out_shape=jax.ShapeDtypeStruct((db, rows, S_DH), F32),
        compiler_params=_cparams(("parallel", "arbitrary")),
        name="dsa_sample_attn",
    )(pt, q32, keys, tkeys, thr, cut, kv_new_pad, *([pool_kv] * P))


def _diff_sample_kernel(pt_ref, lp_ref, gn_ref, q_ref, kn_ref, vn_ref, *rest, P, page, ds, lam_init):
    kpages, vpages = rest[:P], rest[P:2 * P]
    o_ref, m_scr, l_scr, acc_scr = rest[2 * P:]
    c = pl.program_id(1)
    nc = pl.num_programs(1)
    nmap = 2 * DF_HEADS
    rows = ds * nmap
    half = rows // 2

    @pl.when(c == 0)
    def _():
        m_scr[...] = jnp.full(m_scr.shape, -jnp.inf, F32)
        l_scr[...] = jnp.zeros(l_scr.shape, F32)
        acc_scr[...] = jnp.zeros(acc_scr.shape, F32)

    q = q_ref[0] * (DF_DH ** -0.5)
    rr = lax.broadcasted_iota(I32, (rows, DF_WIDTH), 0)
    ll = lax.broadcasted_iota(I32, (rows, DF_WIDTH), 1)
    own = (ll // DF_DH) == (2 * (rr % DF_HEADS) + rr // half)
    qbd = jnp.where(own, jnp.concatenate([q] * nmap, axis=1), 0.0).astype(BF16)

    def attend(kblocks, vblocks, mask, paged):
        if paged:
            s = jnp.concatenate([jnp.dot(qbd, kb.astype(BF16), preferred_element_type=F32)
                                 for kb in kblocks], axis=1)
        else:
            s = jnp.concatenate([lax.dot_general(qbd, kb.astype(BF16), _NT, preferred_element_type=F32)
                                 for kb in kblocks], axis=1)
        if mask is not None:
            s = jnp.where(mask, s, -jnp.inf)
        alpha, p = _softmax_step(s, m_scr, l_scr)
        pv = jnp.zeros((rows, DF_WIDTH), F32)
        for bi, vb in enumerate(vblocks):
            pb = p[:, bi * page:(bi + 1) * page].astype(BF16)
            if paged:
                pv = pv + lax.dot_general(pb, vb.astype(BF16), _NT, preferred_element_type=F32)
            else:
                pv = pv + jnp.dot(pb, vb.astype(BF16), preferred_element_type=F32)
        acc_scr[...] = alpha * acc_scr[...] + pv

    attend([pg[0, 0] for pg in kpages], [pg[0, 0] for pg in vpages], None, True)

    @pl.when(c == nc - 1)
    def _():
        trow = (lax.broadcasted_iota(I32, (rows, page), 0) % half) // DF_HEADS
        tcol = lax.broadcasted_iota(I32, (rows, page), 1)
        attend([kn_ref[0]], [vn_ref[0]], tcol <= trow, False)
        o = acc_scr[...] / l_scr[...]
        hrow = lax.broadcasted_iota(I32, (rows, DF_VDH), 0) % DF_HEADS
        oh = jnp.zeros((rows, DF_VDH), F32)
        for h in range(DF_HEADS):
            oh = jnp.where(hrow == h, o[:, h * DF_VDH:(h + 1) * DF_VDH], oh)
        lam = _diff_lambda(lp_ref, lam_init)
        d = oh[:half, :] - lam * oh[half:, :]
        d = d * lax.rsqrt(jnp.mean(d * d, axis=1, keepdims=True) + EPS) * gn_ref[...]
        o_ref[0] = d * (1.0 - lam_init)


def _diff_sample(pt, lp, gn, q32, k_new_pad, v_new_pad, pool_k, pool_v, layer, lam_init, ds, P):
    db, n_pages = pt.shape
    page = pool_k.shape[3]
    nc = n_pages // P
    rows = ds * 2 * DF_HEADS
    const = lambda a: pl.BlockSpec(a.shape, lambda b, c, pt: (0, 0))
    grid_spec = pltpu.PrefetchScalarGridSpec(
        num_scalar_prefetch=1, grid=(db, nc),
        in_specs=[const(lp), const(gn), _seq_spec(q32.shape[1:]), _seq_spec(k_new_pad.shape[1:]),
                  _seq_spec(v_new_pad.shape[1:])]
        + _page_specs(P, layer, DF_WIDTH, page, P) + _page_specs(P, layer, DF_WIDTH, page, P),
        out_specs=_seq_spec((rows // 2, DF_VDH)),
        scratch_shapes=[pltpu.VMEM((rows, 1), F32), pltpu.VMEM((rows, 1), F32),
                        pltpu.VMEM((rows, DF_WIDTH), F32)])
    return pl.pallas_call(
        functools.partial(_diff_sample_kernel, P=P, page=page, ds=ds, lam_init=lam_init),
        grid_spec=grid_spec, out_shape=jax.ShapeDtypeStruct((db, rows // 2, DF_VDH), F32),
        compiler_params=_cparams(("parallel", "arbitrary")),
        name="diff_sample",
    )(pt, lp, gn, q32, k_new_pad, v_new_pad, *([pool_k] * P), *([pool_v] * P))


def _largest_tile(n, cap):
    t = cap
    while n % t:
        t //= 2
    return t


def _pad_time(a, to, value=0.0):
    return jnp.pad(a, ((0, 0), (0, to - a.shape[1]), (0, 0)), constant_values=value)


def kernel(x_prompt, x_sample, cache_dsa_kv, cache_idx_k, cache_diff_k, cache_diff_v, state_mlstm_C, state_mlstm_n, state_mlstm_m, state_mlstm_conv, page_table, norm_mix, w_in, mlstm_conv_w, mlstm_conv_b, mlstm_gate_b, mlstm_norm, diff_lambda, diff_norm, w_out, norm_ffn, w_up, w_down, norm_final):
    B, T, D = x_prompt.shape
    DB, DS, _ = x_sample.shape
    depth = w_in.shape[0]
    n_pool, page = cache_idx_k.shape[1:3]
    past = page_table.shape[1] * page
    P = _largest_tile(page_table.shape[1], 32)
    LS = LANES

    tm_p = _largest_tile(T, 512)
    tabs_p = _rope_tables(jnp.arange(T), 1)
    tabs_s = _rope_tables(past + jnp.arange(DS), DB)
    gate_pad = jnp.zeros((LANES,), F32).at[G_IG:G_IG + M_HEADS].set(NEG_BIG)
    pool_ik = jnp.swapaxes(cache_idx_k, 2, 3)
    pool_kv = jnp.transpose(cache_dsa_kv, (0, 1, 3, 4, 5, 2)).reshape(depth, n_pool, -1, page)
    pool_dk = jnp.transpose(cache_diff_k, (0, 1, 3, 4, 5, 2)).reshape(depth, n_pool, -1, page)
    pool_dv = jnp.transpose(cache_diff_v, (0, 1, 3, 4, 2)).reshape(depth, n_pool, -1, page)

    xp = x_prompt.reshape(B * T, D)
    xs = x_sample.reshape(DB * DS, D)
    rec_p = [[] for _ in range(8)]
    rec_s = [[] for _ in range(8)]
    for l in range(depth):
        lam_init = 0.8 - 0.6 * math.exp(-0.3 * l)
        last = l == depth - 1
        w_perm = _permute_w_in(w_in[l])
        g_mix = norm_mix[l][None, :]
        gbias = jnp.zeros((1, LANES), F32).at[0, G_IG:G_IG + M_HEADS].set(mlstm_gate_b[l, 0]) \
            .at[0, G_LF:G_LF + M_HEADS].set(mlstm_gate_b[l, 1])
        conv_w, conv_b = mlstm_conv_w[l], mlstm_conv_b[l][None, :]
        nrm_m = mlstm_norm[l][None, :]
        lp, gn = diff_lambda[l], diff_norm[l][None, :]
        wo, wu, wd = w_out[l].astype(BF16), w_up[l].astype(BF16), w_down[l].astype(BF16)
        g_ffn, g_fin = norm_ffn[l][None, :], norm_final[None, :]

        (uqk, mv, mo, sq, skv, iq, dq, dk, dv, ik, gates, ikb, kvb, dkb, dvb) = [
            a.reshape(B, T, -1) for a in _inproj(xp, g_mix, w_perm, gbias, tabs_p, tm_p)]
        gates_t = jnp.swapaxes(gates[:, :, G_IG:G_IG + 16], 1, 2)
        hm, C, n, m = _mlstm(uqk, mv, mo, gates, gates_t, conv_w, conv_b, nrm_m,
                             jnp.zeros((B, CONV_W - 1, 2 * M_WIDTH), F32),
                             jnp.zeros((B, M_HEADS, M_DH, M_DH), F32), jnp.zeros((B, M_HEADS, 1, M_DH), F32),
                             jnp.zeros((B, M_HEADS, 1, 1), F32), _largest_tile(T, 256))
        s_out = _dsa_prompt(sq, iq, gates_t, ikb, kvb, _largest_tile(T, 256))
        hd = _diff_prompt(lp, gn, dq, dkb, dvb, lam_init, _largest_tile(T, 256))
        xp = _out_ffn(xp, hm.reshape(B * T, -1), s_out.reshape(B * T, -1), hd.reshape(B * T, -1),
                      wo, g_ffn, wu, wd, g_fin, last, tm_p)
        for lst, a in zip(rec_p, (skv.reshape(B, T, 2, S_KV_HEADS, S_DH), ik,
                                  dk.reshape(B, T, DF_HEADS, 2, DF_DH), dv.reshape(B, T, DF_HEADS, DF_VDH),
                                  C, n.reshape(B, M_HEADS, M_DH), m.reshape(B, M_HEADS),
                                  uqk[:, T - (CONV_W - 1):, :])):
            lst.append(a)

        (uqk, mv, mo, sq, skv, iq, dq, dk, dv, ik, gates) = [
            a.reshape(DB, DS, -1) for a in _inproj(xs, g_mix, w_perm, gbias, tabs_s, DB * DS)[:11]]
        gates_pd = jnp.concatenate([gates, jnp.broadcast_to(gate_pad, (DB, LS - DS, LANES))], axis=1)
        gates_t = jnp.swapaxes(gates_pd[:, :, G_IG:G_IG + 16], 1, 2)
        hm, C, n, m = _mlstm(_pad_time(uqk, LS), _pad_time(mv, LS), _pad_time(mo, LS), gates_pd, gates_t,
                             conv_w, conv_b, nrm_m, state_mlstm_conv[l], state_mlstm_C[l],
                             state_mlstm_n[l][:, :, None, :], state_mlstm_m[l][:, :, None, None], LS)
        hm = hm[:, :DS, :]
        keys, tkeys = _dsa_sample_score(
            page_table, iq.reshape(DB, DS * IDX_HEADS, IDX_DH),
            gates[:, :, G_IW:G_IW + IDX_HEADS].reshape(DB, DS * IDX_HEADS, 1),
            _pad_time(ik, page), pool_ik, l, DS, P)
        thr, cut = _dsa_sample_thresh(keys.reshape(DB * DS, -1), tkeys.reshape(DB * DS, -1), DS)
        s_out = _dsa_sample_attn(page_table, sq.reshape(DB, DS * S_HEADS, S_DH), keys, tkeys,
                                 thr.reshape(DB, DS, LANES), cut.reshape(DB, DS, LANES),
                                 _pad_time(skv, page), pool_kv, l, DS, P)
        dq32 = dq.reshape(DB, DS, DF_HEADS, 2, DF_DH).transpose(0, 3, 1, 2, 4).reshape(DB, -1, DF_DH)
        hd = _diff_sample(page_table, lp, gn, dq32, _pad_time(dk, page), _pad_time(dv, page),
                          pool_dk, pool_dv, l, lam_init, DS, P)
        xs = _out_ffn(xs, hm.reshape(DB * DS, -1), s_out.reshape(DB * DS, -1), hd.reshape(DB * DS, -1),
                      wo, g_ffn, wu, wd, g_fin, last, DB * DS)
        for lst, a in zip(rec_s, (skv.reshape(DB, DS, 2, S_KV_HEADS, S_DH), ik,
                                  dk.reshape(DB, DS, DF_HEADS, 2, DF_DH), dv.reshape(DB, DS, DF_HEADS, DF_VDH),
                                  C, n.reshape(DB, M_HEADS, M_DH), m.reshape(DB, M_HEADS),
                                  jnp.concatenate([state_mlstm_conv[l], uqk], axis=1)[:, DS:, :])):
            lst.append(a)

    outs_p = [jnp.stack(a) for a in rec_p]
    outs_s = [jnp.stack(a) for a in rec_s]
    return (xp.reshape(B, T, D), xs.reshape(DB, DS, D), *outs_p, *outs_s)
```
